```python
import jax, jax.numpy as jnp
from jax import lax
import numpy as np

D_MODEL = 1024
BATCH = 8
SEQ = 4096
DEPTH = 2

N_Q_HEADS = 16
N_KV_HEADS = 2
HEAD_DIM = 64
WINDOW = 128
ATT_BLOCK = 128
GLA_HEADS = 4
GLA_DK = 128
GLA_DV = 256
GLA_RANK = 16
GLA_GATE_NORMALIZER = 16.0
GLA_CHUNK = 16
D_FF = 2816
N_MOD = 9
EPS = 1e-6

ATT_Q_W = N_Q_HEADS * HEAD_DIM
ATT_KV_W = N_KV_HEADS * HEAD_DIM
GLA_K_W = GLA_HEADS * GLA_DK
GLA_V_W = GLA_HEADS * GLA_DV
IN_SPLIT_SIZES = (ATT_Q_W, ATT_KV_W, ATT_KV_W, GLA_K_W, GLA_K_W, GLA_V_W, GLA_RANK, GLA_V_W, D_MODEL, D_MODEL)
IN_COLS = ATT_Q_W + 2 * ATT_KV_W + 2 * GLA_K_W + 2 * GLA_V_W + GLA_RANK + 2 * D_MODEL

kernel_name = 'hybrid_swa_gla_macaron_adaln'


def rmsnorm(x, gain):
    xf = x.astype(jnp.float32)
    y = xf * lax.rsqrt(jnp.mean(xf * xf, axis=-1, keepdims=True) + EPS)
    return (y * gain.astype(jnp.float32)).astype(x.dtype)


def modulate(h, shift, scale):
    return h * (1 + scale[:, None, :]) + shift[:, None, :]


def swiglu(h, w_in, w_out):
    gate, up = jnp.split(h @ w_in, 2, axis=-1)
    return (jax.nn.silu(gate) * up) @ w_out


def sliding_window_attention(q, k, v, sinks):
    B, S = q.shape[:2]
    nb = S // ATT_BLOCK
    grp = N_Q_HEADS // N_KV_HEADS
    qb = q.reshape(B, nb, ATT_BLOCK, N_KV_HEADS, grp, HEAD_DIM)

    def band(t):
        tb = t.reshape(B, nb, ATT_BLOCK, N_KV_HEADS, HEAD_DIM)
        prev = jnp.concatenate([jnp.zeros_like(tb[:, :1]), tb[:, :-1]], axis=1)
        return jnp.concatenate([prev, tb], axis=2)

    kb, vb = band(k), band(v)
    s = jnp.einsum('bnqhgd,bnkhd->bnhgqk', qb, kb).astype(jnp.float32) * (HEAD_DIM ** -0.5)
    q_pos = jnp.arange(ATT_BLOCK)[:, None] + ATT_BLOCK
    k_pos = jnp.arange(2 * ATT_BLOCK)[None, :]
    dist = q_pos - k_pos
    in_window = (dist >= 0) & (dist < WINDOW)
    key_abs = (jnp.arange(nb)[:, None] - 1) * ATT_BLOCK + k_pos
    mask = in_window[None] & (key_abs >= 0)[:, None, :]
    s = jnp.where(mask[None, :, None, None], s, -jnp.inf)
    sink = sinks.astype(jnp.float32).reshape(N_KV_HEADS, grp)[None, None, :, :, None, None]
    m = jnp.maximum(jnp.max(s, axis=-1, keepdims=True), sink)
    p = jnp.exp(s - m)
    p = p / (jnp.sum(p, axis=-1, keepdims=True) + jnp.exp(sink - m))
    o = jnp.einsum('bnhgqk,bnkhd->bnqhgd', p.astype(v.dtype), vb)
    return o.reshape(B, S, N_Q_HEADS * HEAD_DIM)


def gated_linear_attention(q, k, v, log_a):
    B, S, H, dk = q.shape
    nc = S // GLA_CHUNK

    def chunks(t):
        return t.reshape(B, nc, GLA_CHUNK, H, t.shape[-1]).astype(jnp.float32)

    q = chunks(q) * (dk ** -0.5)
    k, v, g = chunks(k), chunks(v), chunks(log_a)
    G = jnp.cumsum(g, axis=2)
    G_first = G[:, :, :1]
    G_last = G[:, :, -1:]
    q_rel = q * jnp.exp(G - G_first)
    k_rel = k * jnp.exp(G_first - G)
    A = jnp.einsum('bnihd,bnjhd->bnhij', q_rel, k_rel)
    causal = jnp.tril(jnp.ones((GLA_CHUNK, GLA_CHUNK), dtype=bool))
    A = jnp.where(causal, A, 0.0)
    o_intra = jnp.einsum('bnhij,bnjhv->bnihv', A, v)
    qg = q * jnp.exp(G)
    kg = k * jnp.exp(G_last - G)
    decay = jnp.exp(G_last[:, :, 0])

    def step(state, inp):
        qg_c, kg_c, v_c, dec = inp
        o = jnp.einsum('bihd,bhdv->bihv', qg_c, state)
        state = dec[..., None] * state + jnp.einsum('bjhd,bjhv->bhdv', kg_c, v_c)
        return state, o

    state0 = jnp.zeros((B, H, dk, v.shape[-1]), jnp.float32)
    xs = (jnp.moveaxis(qg, 1, 0), jnp.moveaxis(kg, 1, 0), jnp.moveaxis(v, 1, 0), jnp.moveaxis(decay, 1, 0))
    _, o_inter = lax.scan(step, state0, xs)
    o = o_intra + jnp.moveaxis(o_inter, 0, 1)
    return o.reshape(B, S, H, v.shape[-1])


def token_mixer(h, w_in, sinks, w_gate, b_gate, gla_norm, w_out):
    B, S, _ = h.shape
    split_at = [int(i) for i in np.cumsum(IN_SPLIT_SIZES)[:-1]]
    qa, ka, va, qb, kb, vb, lr, r, ga, gb = jnp.split(h @ w_in, split_at, axis=-1)
    ya = sliding_window_attention(qa.reshape(B, S, N_Q_HEADS, HEAD_DIM),
                                  ka.reshape(B, S, N_KV_HEADS, HEAD_DIM),
                                  va.reshape(B, S, N_KV_HEADS, HEAD_DIM), sinks)
    log_a = jax.nn.log_sigmoid((lr @ w_gate + b_gate).astype(jnp.float32)) / GLA_GATE_NORMALIZER
    ob = gated_linear_attention(qb.reshape(B, S, GLA_HEADS, GLA_DK),
                                kb.reshape(B, S, GLA_HEADS, GLA_DK),
                                vb.reshape(B, S, GLA_HEADS, GLA_DV),
                                log_a.reshape(B, S, GLA_HEADS, GLA_DK))
    yb = rmsnorm(ob, gla_norm).reshape(B, S, GLA_V_W).astype(h.dtype) * jax.nn.silu(r)
    merged = jax.nn.sigmoid(ga) * ya + jax.nn.sigmoid(gb) * yb
    return merged @ w_out


def setup_inputs(seed: int = 0) -> dict:
    key = jax.random.key(seed)
    ks = jax.random.split(key, 20)

    def w(k, shape, fan_in, gain=1.0):
        return jax.random.normal(k, shape, jnp.float32) * (gain * fan_in ** -0.5)

    def norm_gain(k, shape):
        return 1.0 + 0.02 * jax.random.normal(k, shape, jnp.float32)

    L = DEPTH
    return {
        'x': jax.random.normal(ks[0], (BATCH, SEQ, D_MODEL), jnp.float32),
        'c': jax.random.normal(ks[1], (BATCH, D_MODEL), jnp.float32),
        'ada_w': w(ks[2], (L, D_MODEL, N_MOD * D_MODEL), D_MODEL, 0.5),
        'ada_b': 0.02 * jax.random.normal(ks[3], (L, N_MOD * D_MODEL), jnp.float32),
        'norm1': norm_gain(ks[4], (L, D_MODEL)),
        'ffn1_w_in': w(ks[5], (L, D_MODEL, 2 * D_FF), D_MODEL),
        'ffn1_w_out': w(ks[6], (L, D_FF, D_MODEL), D_FF),
        'norm_mix': norm_gain(ks[7], (L, D_MODEL)),
        'mix_w_in': w(ks[8], (L, D_MODEL, IN_COLS), D_MODEL),
        'attn_sinks': jax.random.normal(ks[9], (L, N_Q_HEADS), jnp.float32),
        'gla_w_gate': w(ks[10], (L, GLA_RANK, GLA_K_W), GLA_RANK),
        'gla_b_gate': 0.1 * jax.random.normal(ks[11], (L, GLA_K_W), jnp.float32),
        'gla_out_norm': norm_gain(ks[12], (L, GLA_DV)),
        'mix_w_out': w(ks[13], (L, D_MODEL, D_MODEL), D_MODEL),
        'norm2': norm_gain(ks[14], (L, D_MODEL)),
        'ffn2_w_in': w(ks[15], (L, D_MODEL, 2 * D_FF), D_MODEL),
        'ffn2_w_out': w(ks[16], (L, D_FF, D_MODEL), D_FF),
        'final_norm': norm_gain(ks[17], (D_MODEL,)),
    }


def reference(x, c, ada_w, ada_b, norm1, ffn1_w_in, ffn1_w_out, norm_mix, mix_w_in,
              attn_sinks, gla_w_gate, gla_b_gate, gla_out_norm, mix_w_out, norm2,
              ffn2_w_in, ffn2_w_out, final_norm):
    cond = jax.nn.silu(c)
    for l in range(DEPTH):
        mod = cond @ ada_w[l] + ada_b[l]
        sh1, sc1, g1, sh2, sc2, g2, sh3, sc3, g3 = jnp.split(mod, N_MOD, axis=-1)
        h = modulate(rmsnorm(x, norm1[l]), sh1, sc1)
        x = x + 0.5 * g1[:, None, :] * swiglu(h, ffn1_w_in[l], ffn1_w_out[l])
        h = modulate(rmsnorm(x, norm_mix[l]), sh2, sc2)
        x = x + g2[:, None, :] * token_mixer(h, mix_w_in[l], attn_sinks[l], gla_w_gate[l],
                                             gla_b_gate[l], gla_out_norm[l], mix_w_out[l])
        h = modulate(rmsnorm(x, norm2[l]), sh3, sc3)
        x = x + 0.5 * g3[:, None, :] * swiglu(h, ffn2_w_in[l], ffn2_w_out[l])
    return rmsnorm(x, final_norm)
```

```python
import functools

import jax
import jax.numpy as jnp
from jax import lax
from jax.experimental import pallas as pl
from jax.experimental.pallas import tpu as pltpu

D_MODEL = 1024
N_Q_HEADS = 16
N_KV_HEADS = 2
HEAD_DIM = 64
WINDOW = 128
GLA_HEADS = 4
GLA_DK = 128
GLA_DV = 256
GLA_RANK = 16
GLA_GATE_NORMALIZER = 16.0
GLA_SUB = 16
D_FF = 2816
N_MOD = 9
EPS = 1e-6

ATT_Q_W = N_Q_HEADS * HEAD_DIM
ATT_KV_W = N_KV_HEADS * HEAD_DIM
GLA_K_W = GLA_HEADS * GLA_DK
GLA_V_W = GLA_HEADS * GLA_DV

LANES = 128
SEQ_BLOCK = 128
ROW_TILE = 512
FF_CHUNK = 1408
VMEM_LIMIT = 56 * 1024 * 1024

_MXU = jnp.bfloat16
_F32 = jnp.float32

_C_QA = 0
_C_KA = _C_QA + ATT_Q_W
_C_VA = _C_KA + ATT_KV_W
_C_QB = _C_VA + ATT_KV_W
_C_KB = _C_QB + GLA_K_W
_C_VB = _C_KB + GLA_K_W
_C_R = _C_VB + GLA_V_W
_C_GA = _C_R + GLA_V_W
_C_GB = _C_GA + D_MODEL
_C_LR = _C_GB + D_MODEL
_C_END = _C_LR + LANES


def _dot(a, b):
    return jnp.dot(a, b, preferred_element_type=_F32)


def _dot_nt(a, b):
    return lax.dot_general(a, b, (((1,), (1,)), ((), ())), preferred_element_type=_F32)


def _dot_tn(a, b):
    return lax.dot_general(a, b, (((0,), (0,)), ((), ())), preferred_element_type=_F32)


def _sigmoid(x):
    return 1.0 / (1.0 + jnp.exp(-x))


def _rms(x, gain):
    ms = jnp.mean(x * x, axis=-1, keepdims=True)
    return x * lax.rsqrt(ms + EPS) * gain


def _params(sem):
    return pltpu.CompilerParams(dimension_semantics=sem, vmem_limit_bytes=VMEM_LIMIT)


def _resident(shape):
    return pl.BlockSpec(shape, lambda *_: (0,) * len(shape), pipeline_mode=pl.Buffered(1))


def _mod_kernel(c_ref, w_ref, b_ref, o_ref):
    c = c_ref[...]
    cond = c * _sigmoid(c)
    o_ref[...] = _dot(cond.astype(_MXU), w_ref[...].astype(_MXU)) + b_ref[...]


def _modulation(c, ada_w, ada_b):
    depth = ada_w.shape[0]
    batch = c.shape[0]
    out = pl.pallas_call(
        _mod_kernel,
        out_shape=jax.ShapeDtypeStruct((depth, N_MOD, batch, D_MODEL), _F32),
        grid=(depth, N_MOD),
        in_specs=[
            pl.BlockSpec((batch, D_MODEL), lambda l, j: (0, 0)),
            pl.BlockSpec((None, D_MODEL, D_MODEL), lambda l, j: (l, 0, j)),
            pl.BlockSpec((None, None, 1, D_MODEL), lambda l, j: (l, j, 0, 0)),
        ],
        out_specs=pl.BlockSpec((None, None, batch, D_MODEL), lambda l, j: (l, j, 0, 0)),
        compiler_params=_params(("arbitrary", "arbitrary")),
        name="adaln_mod",
    )(c, ada_w, ada_b.reshape(depth, N_MOD, 1, D_MODEL))
    return jnp.transpose(out, (0, 2, 1, 3))


def _modulated_norm(x, mod_ref, gain_ref, piece):
    sh = mod_ref[3 * piece:3 * piece + 1, :]
    sc = mod_ref[3 * piece + 1:3 * piece + 2, :]
    return _rms(x, gain_ref[...]) * (1.0 + sc) + sh


def _ffn_kernel(x_ref, mod_ref, gain_ref, win_ref, wout_ref, *rest, piece, final):
    o_ref = rest[-1]
    x = x_ref[...]
    hb = _modulated_norm(x, mod_ref, gain_ref, piece).astype(_MXU)
    acc = None
    for j in range(D_FF // FF_CHUNK):
        lo = j * FF_CHUNK
        gate = _dot(hb, win_ref[:, lo:lo + FF_CHUNK])
        up = _dot(hb, win_ref[:, D_FF + lo:D_FF + lo + FF_CHUNK])
        act = (gate * _sigmoid(gate) * up).astype(_MXU)
        part = _dot(act, wout_ref[lo:lo + FF_CHUNK, :])
        acc = part if acc is None else acc + part
    g = mod_ref[3 * piece + 2:3 * piece + 3, :]
    y = x + (0.5 * g) * acc
    if final:
        y = _rms(y, rest[0][...])
    o_ref[...] = y


def _ffn(x, mod_l, gain, w_in, w_out, piece, final_gain=None):
    batch, seq, _ = x.shape
    final = final_gain is not None
    in_specs = [
        pl.BlockSpec((None, ROW_TILE, D_MODEL), lambda b, i: (b, i, 0)),
        pl.BlockSpec((None, N_MOD, D_MODEL), lambda b, i: (b, 0, 0)),
        _resident((1, D_MODEL)),
        _resident((D_MODEL, 2 * D_FF)),
        _resident((D_FF, D_MODEL)),
    ]
    args = [x, mod_l, gain.reshape(1, D_MODEL), w_in, w_out]
    if final:
        in_specs.append(_resident((1, D_MODEL)))
        args.append(final_gain.reshape(1, D_MODEL))
    return pl.pallas_call(
        functools.partial(_ffn_kernel, piece=piece, final=final),
        out_shape=jax.ShapeDtypeStruct(x.shape, _F32),
        grid=(batch, seq // ROW_TILE),
        in_specs=in_specs,
        out_specs=pl.BlockSpec((None, ROW_TILE, D_MODEL), lambda b, i: (b, i, 0)),
        compiler_params=_params(("arbitrary", "arbitrary")),
        name="ffn_final" if final else "ffn",
    )(*args)


def _inproj_kernel(x_ref, mod_ref, gain_ref, w_ref, wg_ref, bg_ref,
                   qa_ref, ka_ref, kas_ref, va_ref, vas_ref,
                   qb_ref, kb_ref, vb_ref, g_ref, sr_ref, sga_ref, sgb_ref):
    hb = _modulated_norm(x_ref[...], mod_ref, gain_ref, 1).astype(_MXU)

    def proj(lo, width):
        return _dot(hb, w_ref[:, lo:lo + width])

    qa_ref[...] = proj(_C_QA, ATT_Q_W).astype(qa_ref.dtype)
    ka = proj(_C_KA, ATT_KV_W) * (HEAD_DIM ** -0.5)
    ka_ref[...] = ka.astype(ka_ref.dtype)
    kas_ref[...] = pltpu.roll(ka, HEAD_DIM, axis=1).astype(kas_ref.dtype)
    va = proj(_C_VA, ATT_KV_W)
    va_ref[...] = va.astype(va_ref.dtype)
    vas_ref[...] = pltpu.roll(va, HEAD_DIM, axis=1).astype(vas_ref.dtype)
    qb_ref[...] = proj(_C_QB, GLA_K_W)
    kb_ref[...] = proj(_C_KB, GLA_K_W)
    vb_ref[...] = proj(_C_VB, GLA_V_W).astype(vb_ref.dtype)
    lr = proj(_C_LR, LANES).astype(_MXU)
    z = _dot(lr, wg_ref[...]) + bg_ref[...]
    log_sig = jnp.minimum(z, 0.0) - jnp.log1p(jnp.exp(-jnp.abs(z)))
    g_ref[...] = log_sig / GLA_GATE_NORMALIZER
    r = proj(_C_R, GLA_V_W)
    sr_ref[...] = r * _sigmoid(r)
    sga_ref[...] = _sigmoid(proj(_C_GA, D_MODEL))
    sgb_ref[...] = _sigmoid(proj(_C_GB, D_MODEL))


def _inproj(x, mod_l, gain, w_all, w_gate_p, b_gate):
    batch, seq, _ = x.shape

    def tok(width):
        return pl.BlockSpec((None, ROW_TILE, width), lambda b, i: (b, i, 0))

    def out(width, dtype):
        return jax.ShapeDtypeStruct((batch, seq, width), dtype)

    outs = [
        (ATT_Q_W, _MXU), (ATT_KV_W, _MXU), (ATT_KV_W, _MXU), (ATT_KV_W, _MXU), (ATT_KV_W, _MXU),
        (GLA_K_W, _F32), (GLA_K_W, _F32), (GLA_V_W, _MXU), (GLA_K_W, _F32),
        (GLA_V_W, _F32), (D_MODEL, _F32), (D_MODEL, _F32),
    ]
    return pl.pallas_call(
        _inproj_kernel,
        out_shape=[out(w, d) for w, d in outs],
        grid=(batch, seq // ROW_TILE),
        in_specs=[
            tok(D_MODEL),
            pl.BlockSpec((None, N_MOD, D_MODEL), lambda b, i: (b, 0, 0)),
            _resident((1, D_MODEL)),
            _resident((D_MODEL, _C_END)),
            _resident((LANES, GLA_K_W)),
            _resident((1, GLA_K_W)),
        ],
        out_specs=[tok(w) for w, _ in outs],
        compiler_params=_params(("arbitrary", "arbitrary")),
        name="mix_inproj",
    )(x, mod_l, gain.reshape(1, D_MODEL), w_all, w_gate_p, b_gate.reshape(1, GLA_K_W))


def _swa_kernel(sinks_ref, q_ref, kp_ref, kc_ref, ksp_ref, ksc_ref,
                vp_ref, vc_ref, vsp_ref, vsc_ref, o_ref):
    n = pl.program_id(1)
    grp = N_Q_HEADS // N_KV_HEADS
    pairs = grp // 2
    rows = pairs * SEQ_BLOCK
    band = 2 * SEQ_BLOCK

    kb = jnp.concatenate([kp_ref[...], kc_ref[...]], axis=0)
    kbs = jnp.concatenate([ksp_ref[...], ksc_ref[...]], axis=0)
    vb = jnp.concatenate([vp_ref[...], vc_ref[...]], axis=0)
    vbs = jnp.concatenate([vsp_ref[...], vsc_ref[...]], axis=0)
    low = lax.broadcasted_iota(jnp.int32, (band, LANES), 1) < HEAD_DIM
    zero = jnp.zeros_like(kb)

    qi = lax.broadcasted_iota(jnp.int32, (rows, band), 0) % SEQ_BLOCK
    kj = lax.broadcasted_iota(jnp.int32, (rows, band), 1)
    valid = (kj > qi) & (kj <= qi + WINDOW) & ((kj >= SEQ_BLOCK) | (n > 0))

    for kvh in range(N_KV_HEADS):
        nat, swp = (kb, kbs) if kvh == 0 else (kbs, kb)
        vnat, vswp = (vb, vbs) if kvh == 0 else (vbs, vb)
        k_lo = jnp.where(low, nat, zero)
        k_hi = jnp.where(low, zero, swp)
        v_lo = jnp.where(low, vnat, zero)
        v_hi = jnp.where(low, zero, vswp)
        q = jnp.concatenate(
            [q_ref[:, (kvh * pairs + p) * LANES:(kvh * pairs + p + 1) * LANES] for p in range(pairs)],
            axis=0)
        out = None
        for parity, (k_m, v_m) in enumerate(((k_lo, v_lo), (k_hi, v_hi))):
            s = jnp.where(valid, _dot_nt(q, k_m), -jnp.inf)
            sink = jnp.concatenate(
                [jnp.full((SEQ_BLOCK, 1), sinks_ref[kvh * grp + 2 * p + parity], _F32)
                 for p in range(pairs)], axis=0)
            m = jnp.maximum(jnp.max(s, axis=-1, keepdims=True), sink)
            e = jnp.exp(s - m)
            den = jnp.sum(e, axis=-1, keepdims=True) + jnp.exp(sink - m)
            p_n = (e * (1.0 / den)).astype(_MXU)
            part = _dot(p_n, v_m)
            out = part if out is None else out + part
        for p in range(pairs):
            col = (kvh * pairs + p) * LANES
            o_ref[:, col:col + LANES] = out[p * SEQ_BLOCK:(p + 1) * SEQ_BLOCK, :]


def _swa(sinks, qa, ka, kas, va, vas):
    batch, seq, _ = qa.shape

    def cur(width):
        return pl.BlockSpec((None, SEQ_BLOCK, width), lambda b, n: (b, n, 0))

    def prev(width):
        return pl.BlockSpec((None, SEQ_BLOCK, width), lambda b, n: (b, jnp.maximum(n - 1, 0), 0))

    return pl.pallas_call(
        _swa_kernel,
        out_shape=jax.ShapeDtypeStruct((batch, seq, ATT_Q_W), _F32),
        grid=(batch, seq // SEQ_BLOCK),
        in_specs=[
            pl.BlockSpec(memory_space=pltpu.SMEM),
            cur(ATT_Q_W),
            prev(ATT_KV_W), cur(ATT_KV_W), prev(ATT_KV_W), cur(ATT_KV_W),
            prev(ATT_KV_W), cur(ATT_KV_W), prev(ATT_KV_W), cur(ATT_KV_W),
        ],
        out_specs=cur(ATT_Q_W),
        compiler_params=_params(("arbitrary", "arbitrary")),
        name="swa",
    )(sinks, qa, ka, ka, kas, kas, va, va, vas, vas)


def _rows_bcast(x, starts, reps):
    return jnp.concatenate(
        [jnp.broadcast_to(x[s:s + 1, :], (reps, x.shape[1])) for s in starts], axis=0)


def _gla_kernel(q_ref, k_ref, v_ref, g_ref, gain_ref, o_ref, state_ref):
    n = pl.program_id(1)
    C = SEQ_BLOCK

    @pl.when(n == 0)
    def _():
        state_ref[...] = jnp.zeros_like(state_ref)

    ri = lax.broadcasted_iota(jnp.int32, (C, C), 0)
    ci = lax.broadcasted_iota(jnp.int32, (C, C), 1)

    tri = jnp.where(ci <= ri, 1.0, 0.0).astype(_MXU)
    g = g_ref[...]
    g1 = g.astype(_MXU)
    r1 = g - g1.astype(_F32)
    g2 = r1.astype(_MXU)
    g3 = (r1 - g2.astype(_F32)).astype(_MXU)
    G_all = _dot(tri, g1) + _dot(tri, g2) + _dot(tri, g3)

    diag_mask = (ri // GLA_SUB == ci // GLA_SUB) & (ci <= ri)
    levels = []
    b = GLA_SUB
    while b < C:
        pair_mask = (ri // (2 * b) == ci // (2 * b)) & ((ri // b) % 2 == 1) & ((ci // b) % 2 == 0)
        right = (lax.broadcasted_iota(jnp.int32, (C, GLA_DK), 0) // b) % 2 == 1
        levels.append((b, pair_mask, right))
        b *= 2

    for h in range(GLA_HEADS):
        ks = slice(h * GLA_DK, (h + 1) * GLA_DK)
        vs = slice(h * GLA_DV, (h + 1) * GLA_DV)
        G = G_all[:, ks]
        q = q_ref[:, ks] * (GLA_DK ** -0.5)
        k = k_ref[:, ks]
        v = v_ref[:, vs]
        G_last = G[C - 1:C, :]

        G_first = _rows_bcast(G, range(0, C, GLA_SUB), GLA_SUB)
        a = jnp.where(
            diag_mask,
            _dot_nt((q * jnp.exp(G - G_first)).astype(_MXU), (k * jnp.exp(G_first - G)).astype(_MXU)),
            0.0)
        for b, pair_mask, right in levels:
            ref = _rows_bcast(G, range(b - 1, C, 2 * b), 2 * b)
            e = jnp.exp(jnp.where(right, G - ref, ref - G))
            a = a + jnp.where(pair_mask, _dot_nt((q * e).astype(_MXU), (k * e).astype(_MXU)), 0.0)

        state = state_ref[h]
        qg = q * jnp.exp(G)
        lhs = jnp.concatenate([a.astype(_MXU), qg.astype(_MXU)], axis=1)
        rhs = jnp.concatenate([v, state.astype(_MXU)], axis=0)
        o = _dot(lhs, rhs)

        kg = k * jnp.exp(G_last - G)
        upd = _dot_tn(kg.astype(_MXU), v)
        dec_t = jnp.transpose(jnp.broadcast_to(jnp.exp(G_last), (C, GLA_DK)))
        state_ref[h] = state * jnp.concatenate([dec_t] * (GLA_DV // GLA_DK), axis=1) + upd

        o_ref[:, vs] = _rms(o, gain_ref[...])


def _gla(qb, kb, vb, g, gain):
    batch, seq, _ = qb.shape

    def tok(width):
        return pl.BlockSpec((None, SEQ_BLOCK, width), lambda b, n: (b, n, 0))

    return pl.pallas_call(
        _gla_kernel,
        out_shape=jax.ShapeDtypeStruct((batch, seq, GLA_V_W), _F32),
        grid=(batch, seq // SEQ_BLOCK),
        in_specs=[tok(GLA_K_W), tok(GLA_K_W), tok(GLA_V_W), tok(GLA_K_W), _resident((1, GLA_DV))],
        out_specs=tok(GLA_V_W),
        scratch_shapes=[pltpu.VMEM((GLA_HEADS, GLA_DK, GLA_DV), _F32)],
        compiler_params=_params(("arbitrary", "arbitrary")),
        name="gla",
    )(qb, kb, vb, g, gain.reshape(1, GLA_DV))


def _merge_kernel(x_ref, mod_ref, ya_ref, yb_ref, sr_ref, sga_ref, sgb_ref, w_ref, o_ref):
    merged = sga_ref[...] * ya_ref[...] + sgb_ref[...] * (yb_ref[...] * sr_ref[...])
    out = _dot(merged.astype(_MXU), w_ref[...])
    o_ref[...] = x_ref[...] + mod_ref[5:6, :] * out


def _merge(x, mod_l, ya, yb, sr, sga, sgb, w_out):
    batch, seq, _ = x.shape
    tok = pl.BlockSpec((None, ROW_TILE, D_MODEL), lambda b, i: (b, i, 0))
    return pl.pallas_call(
        _merge_kernel,
        out_shape=jax.ShapeDtypeStruct(x.shape, _F32),
        grid=(batch, seq // ROW_TILE),
        in_specs=[tok, pl.BlockSpec((None, N_MOD, D_MODEL), lambda b, i: (b, 0, 0)),
                  tok, tok, tok, tok, tok, _resident((D_MODEL, D_MODEL))],
        out_specs=tok,
        compiler_params=_params(("arbitrary", "arbitrary")),
        name="mix_merge",
    )(x, mod_l, ya, yb, sr, sga, sgb, w_out)


def _arrange_mix_w_in(w):
    sizes = (ATT_Q_W, ATT_KV_W, ATT_KV_W, GLA_K_W, GLA_K_W, GLA_V_W, GLA_RANK, GLA_V_W, D_MODEL, D_MODEL)
    offs = [0]
    for s in sizes:
        offs.append(offs[-1] + s)
    pieces = [w[:, offs[i]:offs[i + 1]] for i in range(len(sizes))]
    lr = jnp.pad(pieces[6], ((0, 0), (0, LANES - GLA_RANK)))
    return jnp.concatenate(pieces[:6] + pieces[7:] + [lr], axis=1).astype(_MXU)


def kernel(x, c, ada_w, ada_b, norm1, ffn1_w_in, ffn1_w_out, norm_mix, mix_w_in, attn_sinks,
           gla_w_gate, gla_b_gate, gla_out_norm, mix_w_out, norm2, ffn2_w_in, ffn2_w_out, final_norm):
    depth = ada_w.shape[0]
    assert x.shape[1] % ROW_TILE == 0 and x.shape[2] == D_MODEL
    mod = _modulation(c, ada_w, ada_b)
    for l in range(depth):
        x = _ffn(x, mod[l], norm1[l], ffn1_w_in[l].astype(_MXU), ffn1_w_out[l].astype(_MXU), 0)
        w_gate_p = jnp.pad(gla_w_gate[l], ((0, LANES - GLA_RANK), (0, 0))).astype(_MXU)
        qa, ka, kas, va, vas, qb, kb, vb, g, sr, sga, sgb = _inproj(
            x, mod[l], norm_mix[l], _arrange_mix_w_in(mix_w_in[l]), w_gate_p, gla_b_gate[l])
        ya = _swa(attn_sinks[l], qa, ka, kas, va, vas)
        yb = _gla(qb, kb, vb, g, gla_out_norm[l])
        x = _merge(x, mod[l], ya, yb, sr, sga, sgb, mix_w_out[l].astype(_MXU))
        x = _ffn(x, mod[l], norm2[l], ffn2_w_in[l].astype(_MXU), ffn2_w_out[l].astype(_MXU), 2,
                 final_gain=final_norm if l == depth - 1 else None)
    return x
```

```python
import functools

import jax
import jax.numpy as jnp
from jax import lax
from jax.experimental import pallas as pl
from jax.experimental.pallas import tpu as pltpu

D_MODEL = 1024
N_Q_HEADS = 16
N_KV_HEADS = 2
HEAD_DIM = 64
WINDOW = 128
GLA_HEADS = 4
GLA_DK = 128
GLA_DV = 256
GLA_RANK = 16
GLA_GATE_NORMALIZER = 16.0
GLA_SUB = 16
D_FF = 2816
N_MOD = 9
EPS = 1e-6

ATT_Q_W = N_Q_HEADS * HEAD_DIM
ATT_KV_W = N_KV_HEADS * HEAD_DIM
GLA_K_W = GLA_HEADS * GLA_DK
GLA_V_W = GLA_HEADS * GLA_DV

LANES = 128
SEQ_BLOCK = 128
GLA_CHUNKS = 4
LOG2_E = 1.4426950408889634
SWA_BLOCKS = 4
ROW_TILE = 512
FF_CHUNK = 1408
VMEM_LIMIT = 56 * 1024 * 1024

_MXU = jnp.bfloat16
_F32 = jnp.float32

_C_QA = 0
_C_KA = _C_QA + ATT_Q_W
_C_VA = _C_KA + ATT_KV_W
_C_QB = _C_VA + ATT_KV_W
_C_KB = _C_QB + GLA_K_W
_C_VB = _C_KB + GLA_K_W
_C_R = _C_VB + GLA_V_W
_C_GA = _C_R + GLA_V_W
_C_GB = _C_GA + D_MODEL
_C_LR = _C_GB + D_MODEL
_C_END = _C_LR + LANES


def _dot(a, b):
    return jnp.dot(a, b, preferred_element_type=_F32)


def _dot_nt(a, b):
    return lax.dot_general(a, b, (((1,), (1,)), ((), ())), preferred_element_type=_F32)


def _dot_tn(a, b):
    return lax.dot_general(a, b, (((0,), (0,)), ((), ())), preferred_element_type=_F32)


def _sigmoid(x):
    return 1.0 / (1.0 + jnp.exp(-x))


def _rms(x, gain):
    ms = jnp.mean(x * x, axis=-1, keepdims=True)
    return x * lax.rsqrt(ms + EPS) * gain


def _params(sem):
    return pltpu.CompilerParams(dimension_semantics=sem, vmem_limit_bytes=VMEM_LIMIT)


def _resident(shape):
    return pl.BlockSpec(shape, lambda *_: (0,) * len(shape), pipeline_mode=pl.Buffered(1))


def _mod_kernel(c_ref, w_ref, b_ref, o_ref):
    c = c_ref[...]
    cond = c * _sigmoid(c)
    o_ref[...] = _dot(cond.astype(_MXU), w_ref[...].astype(_MXU)) + b_ref[...]


def _modulation(c, ada_w, ada_b):
    depth = ada_w.shape[0]
    batch = c.shape[0]
    out = pl.pallas_call(
        _mod_kernel,
        out_shape=jax.ShapeDtypeStruct((depth, N_MOD, batch, D_MODEL), _F32),
        grid=(depth, N_MOD),
        in_specs=[
            pl.BlockSpec((batch, D_MODEL), lambda l, j: (0, 0)),
            pl.BlockSpec((None, D_MODEL, D_MODEL), lambda l, j: (l, 0, j)),
            pl.BlockSpec((None, None, 1, D_MODEL), lambda l, j: (l, j, 0, 0)),
        ],
        out_specs=pl.BlockSpec((None, None, batch, D_MODEL), lambda l, j: (l, j, 0, 0)),
        compiler_params=_params(("arbitrary", "arbitrary")),
        name="adaln_mod",
    )(c, ada_w, ada_b.reshape(depth, N_MOD, 1, D_MODEL))
    return jnp.transpose(out, (0, 2, 1, 3))


def _modulated_norm(x, mod_ref, gain_ref, piece):
    sh = mod_ref[3 * piece:3 * piece + 1, :]
    sc = mod_ref[3 * piece + 1:3 * piece + 2, :]
    return _rms(x, gain_ref[...]) * (1.0 + sc) + sh


def _ffn_kernel(x_ref, mod_ref, gain_ref, win_ref, wout_ref, *rest, piece, final):
    o_ref = rest[-1]
    x = x_ref[...]
    hb = _modulated_norm(x, mod_ref, gain_ref, piece).astype(_MXU)
    acc = None
    for j in range(D_FF // FF_CHUNK):
        lo = j * FF_CHUNK
        gate = _dot(hb, win_ref[:, lo:lo + FF_CHUNK])
        up = _dot(hb, win_ref[:, D_FF + lo:D_FF + lo + FF_CHUNK])
        act = (gate * _sigmoid(gate) * up).astype(_MXU)
        part = _dot(act, wout_ref[lo:lo + FF_CHUNK, :])
        acc = part if acc is None else acc + part
    g = mod_ref[3 * piece + 2:3 * piece + 3, :]
    y = x + (0.5 * g) * acc
    if final:
        y = _rms(y, rest[0][...])
    o_ref[...] = y


def _ffn(x, mod_l, gain, w_in, w_out, piece, final_gain=None):
    batch, seq, _ = x.shape
    final = final_gain is not None
    in_specs = [
        pl.BlockSpec((None, ROW_TILE, D_MODEL), lambda b, i: (b, i, 0)),
        pl.BlockSpec((None, N_MOD, D_MODEL), lambda b, i: (b, 0, 0)),
        _resident((1, D_MODEL)),
        _resident((D_MODEL, 2 * D_FF)),
        _resident((D_FF, D_MODEL)),
    ]
    args = [x, mod_l, gain.reshape(1, D_MODEL), w_in, w_out]
    if final:
        in_specs.append(_resident((1, D_MODEL)))
        args.append(final_gain.reshape(1, D_MODEL))
    return pl.pallas_call(
        functools.partial(_ffn_kernel, piece=piece, final=final),
        out_shape=jax.ShapeDtypeStruct(x.shape, _F32),
        grid=(batch, seq // ROW_TILE),
        in_specs=in_specs,
        out_specs=pl.BlockSpec((None, ROW_TILE, D_MODEL), lambda b, i: (b, i, 0)),
        compiler_params=_params(("arbitrary", "arbitrary")),
        name="ffn_final" if final else "ffn",
    )(*args)


def _inproj_kernel(x_ref, mod_ref, gain_ref, w_ref, wg_ref, bg_ref,
                   qa_ref, ka_ref, kas_ref, vat_ref,
                   qb_ref, kb_ref, vb_ref, g_ref, sr_ref, sga_ref, sgb_ref):
    hb = _modulated_norm(x_ref[...], mod_ref, gain_ref, 1).astype(_MXU)

    def proj(lo, width):
        return _dot(hb, w_ref[:, lo:lo + width])

    qa_ref[...] = proj(_C_QA, ATT_Q_W).astype(qa_ref.dtype)
    ka = proj(_C_KA, ATT_KV_W) * (HEAD_DIM ** -0.5)
    ka_ref[...] = ka.astype(ka_ref.dtype)
    kas_ref[...] = pltpu.roll(ka, HEAD_DIM, axis=1).astype(kas_ref.dtype)
    vat_ref[...] = proj(_C_VA, ATT_KV_W).T.astype(vat_ref.dtype)
    qb_ref[...] = proj(_C_QB, GLA_K_W)
    kb_ref[...] = proj(_C_KB, GLA_K_W)
    vb_ref[...] = proj(_C_VB, GLA_V_W).astype(vb_ref.dtype)
    lr = proj(_C_LR, LANES).astype(_MXU)
    z = _dot(lr, wg_ref[...]) + bg_ref[...]
    log_sig = jnp.minimum(z, 0.0) - jnp.log1p(jnp.exp(-jnp.abs(z)))
    g_ref[...] = log_sig / GLA_GATE_NORMALIZER
    r = proj(_C_R, GLA_V_W)
    sr_ref[...] = r * _sigmoid(r)
    sga_ref[...] = _sigmoid(proj(_C_GA, D_MODEL))
    sgb_ref[...] = _sigmoid(proj(_C_GB, D_MODEL))


def _inproj(x, mod_l, gain, w_all, w_gate_p, b_gate):
    batch, seq, _ = x.shape

    def tok(width):
        return pl.BlockSpec((None, ROW_TILE, width), lambda b, i: (b, i, 0))

    def out(width, dtype):
        return jax.ShapeDtypeStruct((batch, seq, width), dtype)

    outs = [
        (ATT_Q_W, _MXU), (ATT_KV_W, _MXU), (ATT_KV_W, _MXU), None,
        (GLA_K_W, _F32), (GLA_K_W, _F32), (GLA_V_W, _MXU), (GLA_K_W, _F32),
        (GLA_V_W, _F32), (D_MODEL, _F32), (D_MODEL, _F32),
    ]
    vat_shape = jax.ShapeDtypeStruct((batch, ATT_KV_W, seq), _MXU)
    vat_spec = pl.BlockSpec((None, ATT_KV_W, ROW_TILE), lambda b, i: (b, 0, i))
    return pl.pallas_call(
        _inproj_kernel,
        out_shape=[out(*o) if o else vat_shape for o in outs],
        grid=(batch, seq // ROW_TILE),
        in_specs=[
            tok(D_MODEL),
            pl.BlockSpec((None, N_MOD, D_MODEL), lambda b, i: (b, 0, 0)),
            _resident((1, D_MODEL)),
            _resident((D_MODEL, _C_END)),
            _resident((LANES, GLA_K_W)),
            _resident((1, GLA_K_W)),
        ],
        out_specs=[tok(o[0]) if o else vat_spec for o in outs],
        compiler_params=_params(("arbitrary", "arbitrary")),
        name="mix_inproj",
    )(x, mod_l, gain.reshape(1, D_MODEL), w_all, w_gate_p, b_gate.reshape(1, GLA_K_W))


def _swa_kernel(sinks_ref, q_ref, kp_ref, kc_ref, ksp_ref, ksc_ref, vtp_ref, vtc_ref, o_ref):
    n = pl.program_id(1)
    grp = N_Q_HEADS // N_KV_HEADS
    pairs = grp // 2
    band = 2 * SEQ_BLOCK

    k_all = jnp.concatenate([kp_ref[...], kc_ref[...]], axis=0)
    ks_all = jnp.concatenate([ksp_ref[...], ksc_ref[...]], axis=0)
    vt_all = jnp.concatenate([vtp_ref[...], vtc_ref[...]], axis=1)
    low = lax.broadcasted_iota(jnp.int32, (band, LANES), 1) < HEAD_DIM
    zero = jnp.zeros((band, LANES), k_all.dtype)
    ones = jnp.ones((HEAD_DIM, band), vt_all.dtype)

    kj = lax.broadcasted_iota(jnp.int32, (band, SEQ_BLOCK), 0)
    qi = lax.broadcasted_iota(jnp.int32, (band, SEQ_BLOCK), 1)
    in_window = (kj > qi) & (kj <= qi + WINDOW)

    def scores(j, kvh, parity):
        r0 = j * SEQ_BLOCK
        valid = in_window if j > 0 else in_window & ((kj >= SEQ_BLOCK) | (n > 0))
        bias = jnp.concatenate([jnp.where(valid, 0.0, -jnp.inf)] * pairs, axis=1)
        nat, swp = (k_all, ks_all) if kvh == 0 else (ks_all, k_all)
        if parity == 0:
            k_m = jnp.where(low, nat[r0:r0 + band], zero)
        else:
            k_m = jnp.where(low, zero, swp[r0:r0 + band])
        q = jnp.concatenate(
            [q_ref[r0:r0 + SEQ_BLOCK, (kvh * pairs + p) * LANES:(kvh * pairs + p + 1) * LANES]
             for p in range(pairs)], axis=0)
        return _dot_nt(k_m, q) + bias

    def attend(j, kvh, parity, s):
        r0 = j * SEQ_BLOCK
        v_h = vt_all[kvh * HEAD_DIM:(kvh + 1) * HEAD_DIM, r0:r0 + band]
        vt_m = jnp.concatenate([v_h, ones] if parity == 0 else [ones, v_h], axis=0)
        sink = jnp.concatenate(
            [jnp.full((1, SEQ_BLOCK), sinks_ref[kvh * grp + 2 * p + parity], _F32)
             for p in range(pairs)], axis=1)
        m = jnp.maximum(jnp.max(s, axis=0, keepdims=True), sink)
        pv = _dot(vt_m, jnp.exp(s - m).astype(_MXU))
        if parity == 0:
            num, den = pv[:HEAD_DIM], pv[HEAD_DIM:HEAD_DIM + 1]
        else:
            num, den = pv[HEAD_DIM:], pv[:1]
        return num * (1.0 / (den + jnp.exp(sink - m)))

    groups = [(j, kvh, parity) for j in range(SWA_BLOCKS) for kvh in range(N_KV_HEADS) for parity in range(2)]
    s_next = scores(*groups[0])
    halves = []
    for i, (j, kvh, parity) in enumerate(groups):
        s = s_next
        if i + 1 < len(groups):
            s_next = scores(*groups[i + 1])
        halves.append(attend(j, kvh, parity, s))
        if parity == 1:
            out_t = jnp.concatenate(halves, axis=0)
            halves = []
            for p in range(pairs):
                col = (kvh * pairs + p) * LANES
                o_ref[j * SEQ_BLOCK:(j + 1) * SEQ_BLOCK, col:col + LANES] = (
                    out_t[:, p * SEQ_BLOCK:(p + 1) * SEQ_BLOCK].T)


def _swa(sinks, qa, ka, kas, vat):
    batch, seq, _ = qa.shape

    rows = SWA_BLOCKS * SEQ_BLOCK

    def cur(width):
        return pl.BlockSpec((None, rows, width), lambda b, n: (b, n, 0))

    def prev(width):
        return pl.BlockSpec((None, SEQ_BLOCK, width),
                            lambda b, n: (b, jnp.maximum(n * SWA_BLOCKS - 1, 0), 0))

    vt_cur = pl.BlockSpec((None, ATT_KV_W, rows), lambda b, n: (b, 0, n))
    vt_prev = pl.BlockSpec((None, ATT_KV_W, SEQ_BLOCK),
                           lambda b, n: (b, 0, jnp.maximum(n * SWA_BLOCKS - 1, 0)))
    return pl.pallas_call(
        _swa_kernel,
        out_shape=jax.ShapeDtypeStruct((batch, seq, ATT_Q_W), _F32),
        grid=(batch, seq // rows),
        in_specs=[
            pl.BlockSpec(memory_space=pltpu.SMEM),
            cur(ATT_Q_W),
            prev(ATT_KV_W), cur(ATT_KV_W), prev(ATT_KV_W), cur(ATT_KV_W),
            vt_prev, vt_cur,
        ],
        out_specs=cur(ATT_Q_W),
        compiler_params=_params(("arbitrary", "arbitrary")),
        name="swa",
    )(sinks, qa, ka, ka, kas, kas, vat, vat)


def _rows_bcast(x, starts, reps):
    return jnp.concatenate(
        [jnp.broadcast_to(x[s:s + 1, :], (reps, x.shape[1])) for s in starts], axis=0)


def _gla_kernel(q_ref, k_ref, v_ref, g_ref, gain_ref, o_ref, state_ref):
    n = pl.program_id(1)
    C = SEQ_BLOCK

    @pl.when(n == 0)
    def _():
        state_ref[...] = jnp.zeros_like(state_ref)

    ri = lax.broadcasted_iota(jnp.int32, (C, C), 0)
    ci = lax.broadcasted_iota(jnp.int32, (C, C), 1)
    tri = jnp.where(ci <= ri, 1.0, 0.0).astype(_MXU)
    diag_mask = (ri // GLA_SUB == ci // GLA_SUB) & (ci <= ri)
    levels = []
    b = GLA_SUB
    while b < C:
        pair_mask = (ri // (2 * b) == ci // (2 * b)) & ((ri // b) % 2 == 1) & ((ci // b) % 2 == 0)
        right = (lax.broadcasted_iota(jnp.int32, (C, GLA_DK), 0) // b) % 2 == 1
        levels.append((b, pair_mask, right))
        b *= 2

    def cum_decay(c):
        g = g_ref[c * C:(c + 1) * C, :] * LOG2_E
        g1 = g.astype(_MXU)
        r1 = g - g1.astype(_F32)
        g2 = r1.astype(_MXU)
        g3 = (r1 - g2.astype(_F32)).astype(_MXU)
        return _dot(tri, g1) + _dot(tri, g2) + _dot(tri, g3)

    def intra(c, h, G_all):
        rows = slice(c * C, (c + 1) * C)
        ks = slice(h * GLA_DK, (h + 1) * GLA_DK)
        G = G_all[:, ks]
        q = q_ref[rows, ks] * (GLA_DK ** -0.5)
        k = k_ref[rows, ks]
        G_last = G[C - 1:C, :]
        G_first = _rows_bcast(G, range(0, C, GLA_SUB), GLA_SUB)
        a = jnp.where(
            diag_mask,
            _dot_nt((q * jnp.exp2(G - G_first)).astype(_MXU), (k * jnp.exp2(G_first - G)).astype(_MXU)),
            0.0)
        for b, pair_mask, right in levels:
            ref = _rows_bcast(G, range(b - 1, C, 2 * b), 2 * b)
            e = jnp.exp2(jnp.where(right, G - ref, ref - G))
            a = a + jnp.where(pair_mask, _dot_nt((q * e).astype(_MXU), (k * e).astype(_MXU)), 0.0)
        qg = (q * jnp.exp2(G)).astype(_MXU)
        kg = (k * jnp.exp2(G_last - G)).astype(_MXU)
        return a.astype(_MXU), qg, kg, jnp.exp2(G_last)

    def inter(c, h, a, qg, kg, dec):
        rows = slice(c * C, (c + 1) * C)
        vs = slice(h * GLA_DV, (h + 1) * GLA_DV)
        v = v_ref[rows, vs]
        state = state_ref[h]
        lhs = jnp.concatenate([a, qg], axis=1)
        rhs = jnp.concatenate([v, state.astype(_MXU)], axis=0)
        o = _dot(lhs, rhs)
        upd = _dot_tn(kg, v)
        dec_t = jnp.transpose(jnp.broadcast_to(dec, (C, GLA_DK)))
        state_ref[h] = state * jnp.concatenate([dec_t] * (GLA_DV // GLA_DK), axis=1) + upd
        o_ref[rows, vs] = _rms(o, gain_ref[...])

    units = [(c, h) for c in range(GLA_CHUNKS) for h in range(GLA_HEADS)]
    G_all = cum_decay(0)
    nxt = intra(0, 0, G_all)
    for i, (c, h) in enumerate(units):
        cur = nxt
        if i + 1 < len(units):
            c2, h2 = units[i + 1]
            if c2 != c:
                G_all = cum_decay(c2)
            nxt = intra(c2, h2, G_all)
        inter(c, h, *cur)


def _gla(qb, kb, vb, g, gain):
    batch, seq, _ = qb.shape
    rows = GLA_CHUNKS * SEQ_BLOCK

    def tok(width):
        return pl.BlockSpec((None, rows, width), lambda b, n: (b, n, 0))

    return pl.pallas_call(
        _gla_kernel,
        out_shape=jax.ShapeDtypeStruct((batch, seq, GLA_V_W), _F32),
        grid=(batch, seq // rows),
        in_specs=[tok(GLA_K_W), tok(GLA_K_W), tok(GLA_V_W), tok(GLA_K_W), _resident((1, GLA_DV))],
        out_specs=tok(GLA_V_W),
        scratch_shapes=[pltpu.VMEM((GLA_HEADS, GLA_DK, GLA_DV), _F32)],
        compiler_params=_params(("arbitrary", "arbitrary")),
        name="gla",
    )(qb, kb, vb, g, gain.reshape(1, GLA_DV))


def _merge_kernel(x_ref, mod_ref, ya_ref, yb_ref, sr_ref, sga_ref, sgb_ref, w_ref, o_ref):
    merged = sga_ref[...] * ya_ref[...] + sgb_ref[...] * (yb_ref[...] * sr_ref[...])
    out = _dot(merged.astype(_MXU), w_ref[...])
    o_ref[...] = x_ref[...] + mod_ref[5:6, :] * out


def _merge(x, mod_l, ya, yb, sr, sga, sgb, w_out):
    batch, seq, _ = x.shape
    tok = pl.BlockSpec((None, ROW_TILE, D_MODEL), lambda b, i: (b, i, 0))
    return pl.pallas_call(
        _merge_kernel,
        out_shape=jax.ShapeDtypeStruct(x.shape, _F32),
        grid=(batch, seq // ROW_TILE),
        in_specs=[tok, pl.BlockSpec((None, N_MOD, D_MODEL), lambda b, i: (b, 0, 0)),
                  tok, tok, tok, tok, tok, _resident((D_MODEL, D_MODEL))],
        out_specs=tok,
        compiler_params=_params(("arbitrary", "arbitrary")),
        name="mix_merge",
    )(x, mod_l, ya, yb, sr, sga, sgb, w_out)


def _arrange_mix_w_in(w):
    sizes = (ATT_Q_W, ATT_KV_W, ATT_KV_W, GLA_K_W, GLA_K_W, GLA_V_W, GLA_RANK, GLA_V_W, D_MODEL, D_MODEL)
    offs = [0]
    for s in sizes:
        offs.append(offs[-1] + s)
    pieces = [w[:, offs[i]:offs[i + 1]] for i in range(len(sizes))]
    lr = jnp.pad(pieces[6], ((0, 0), (0, LANES - GLA_RANK)))
    return jnp.concatenate(pieces[:6] + pieces[7:] + [lr], axis=1).astype(_MXU)


def kernel(x, c, ada_w, ada_b, norm1, ffn1_w_in, ffn1_w_out, norm_mix, mix_w_in, attn_sinks,
           gla_w_gate, gla_b_gate, gla_out_norm, mix_w_out, norm2, ffn2_w_in, ffn2_w_out, final_norm):
    depth = ada_w.shape[0]
    assert x.shape[1] % ROW_TILE == 0 and x.shape[2] == D_MODEL
    mod = _modulation(c, ada_w, ada_b)
    for l in range(depth):
        x = _ffn(x, mod[l], norm1[l], ffn1_w_in[l].astype(_MXU), ffn1_w_out[l].astype(_MXU), 0)
        w_gate_p = jnp.pad(gla_w_gate[l], ((0, LANES - GLA_RANK), (0, 0))).astype(_MXU)
        qa, ka, kas, vat, qb, kb, vb, g, sr, sga, sgb = _inproj(
            x, mod[l], norm_mix[l], _arrange_mix_w_in(mix_w_in[l]), w_gate_p, gla_b_gate[l])
        ya = _swa(attn_sinks[l], qa, ka, kas, vat)
        yb = _gla(qb, kb, vb, g, gla_out_norm[l])
        x = _merge(x, mod[l], ya, yb, sr, sga, sgb, mix_w_out[l].astype(_MXU))
        x = _ffn(x, mod[l], norm2[l], ffn2_w_in[l].astype(_MXU), ffn2_w_out[l].astype(_MXU), 2,
                 final_gain=final_norm if l == depth - 1 else None)
    return x
```

```python
import functools

import jax
import jax.numpy as jnp
from jax import lax
from jax.experimental import pallas as pl
from jax.experimental.pallas import tpu as pltpu

D_MODEL = 1024
N_Q_HEADS = 16
N_KV_HEADS = 2
HEAD_DIM = 64
WINDOW = 128
GLA_HEADS = 4
GLA_DK = 128
GLA_DV = 256
GLA_RANK = 16
GLA_GATE_NORMALIZER = 16.0
GLA_SUB = 16
D_FF = 2816
N_MOD = 9
EPS = 1e-6

ATT_Q_W = N_Q_HEADS * HEAD_DIM
ATT_KV_W = N_KV_HEADS * HEAD_DIM
GLA_K_W = GLA_HEADS * GLA_DK
GLA_V_W = GLA_HEADS * GLA_DV

LANES = 128
SEQ_BLOCK = 128
GLA_CHUNKS = 4
GLA_PROJ_CHUNKS = 2
LOG2_E = 1.4426950408889634
SWA_BLOCKS = 4
ROW_TILE = 512
IN_SUB = 256
SUB_TILE = 256
FFN_ROWS = 1024
MXU_WIDTH = 256
FF_SPLITS = (0, 6 * MXU_WIDTH, D_FF)
VMEM_LIMIT = 56 * 1024 * 1024

_MXU = jnp.bfloat16
_F32 = jnp.float32

_C_QA = 0
_C_KA = _C_QA + ATT_Q_W
_C_VA = _C_KA + ATT_KV_W
_C_QB = _C_VA + ATT_KV_W
_C_KB = _C_QB + GLA_K_W
_C_VB = _C_KB + GLA_K_W
_C_R = _C_VB + GLA_V_W
_C_GA = _C_R + GLA_V_W
_C_GB = _C_GA + D_MODEL
_C_LR = _C_GB + D_MODEL
_C_END = _C_LR + LANES


def _dot(a, b):
    return jnp.dot(a, b, preferred_element_type=_F32)


def _dot_nt(a, b):
    return lax.dot_general(a, b, (((1,), (1,)), ((), ())), preferred_element_type=_F32)


def _dot_tn(a, b):
    return lax.dot_general(a, b, (((0,), (0,)), ((), ())), preferred_element_type=_F32)


def _sigmoid(x):
    return 1.0 / (1.0 + jnp.exp(-x))


def _rms(x, gain):
    ms = jnp.mean(x * x, axis=-1, keepdims=True)
    return x * lax.rsqrt(ms + EPS) * gain


def _params(sem):
    return pltpu.CompilerParams(dimension_semantics=sem, vmem_limit_bytes=VMEM_LIMIT)


def _resident(shape, layer=None):
    if layer is None:
        return pl.BlockSpec(shape, lambda *_: (0,) * len(shape), pipeline_mode=pl.Buffered(1))
    return pl.BlockSpec((None,) + tuple(shape), lambda *_: (layer,) + (0,) * len(shape),
                        pipeline_mode=pl.Buffered(1))


def _mod_spec(layer):
    return pl.BlockSpec((None, None, N_MOD, D_MODEL), lambda b, i: (layer, b, 0, 0))


def _mod_kernel(c_ref, w_ref, b_ref, o_ref):
    c = c_ref[...]
    cond = c * _sigmoid(c)
    o_ref[...] = _dot(cond.astype(_MXU), w_ref[...].astype(_MXU)) + b_ref[...]


def _modulation(c, ada_w, ada_b):
    depth = ada_w.shape[0]
    batch = c.shape[0]
    out = pl.pallas_call(
        _mod_kernel,
        out_shape=jax.ShapeDtypeStruct((depth, N_MOD, batch, D_MODEL), _F32),
        grid=(depth, N_MOD),
        in_specs=[
            pl.BlockSpec((batch, D_MODEL), lambda l, j: (0, 0)),
            pl.BlockSpec((None, D_MODEL, D_MODEL), lambda l, j: (l, 0, j)),
            pl.BlockSpec((None, None, 1, D_MODEL), lambda l, j: (l, j, 0, 0)),
        ],
        out_specs=pl.BlockSpec((None, None, batch, D_MODEL), lambda l, j: (l, j, 0, 0)),
        compiler_params=_params(("arbitrary", "arbitrary")),
        name="adaln_mod",
    )(c, ada_w, ada_b.reshape(depth, N_MOD, 1, D_MODEL))
    return jnp.transpose(out, (0, 2, 1, 3))


def _modulated_norm(x, mod_ref, gain_ref, piece):
    sh = mod_ref[3 * piece:3 * piece + 1, :]
    sc = mod_ref[3 * piece + 1:3 * piece + 2, :]
    return _rms(x, gain_ref[...]) * (1.0 + sc) + sh


def _ffn_kernel(x_ref, mod_ref, gain_ref, win_ref, wout_ref, *rest, piece, final):
    o_ref = rest[-1]
    g = mod_ref[3 * piece + 2:3 * piece + 3, :]
    subs = [slice(s * SUB_TILE, (s + 1) * SUB_TILE) for s in range(FFN_ROWS // SUB_TILE)]
    hbs = [_modulated_norm(x_ref[rows, :], mod_ref, gain_ref, piece).astype(_MXU) for rows in subs]
    for rows, hb in zip(subs, hbs):
        acc = None
        for lo, hi in zip(FF_SPLITS[:-1], FF_SPLITS[1:]):
            gate = _dot(hb, win_ref[:, lo:hi])
            up = _dot(hb, win_ref[:, D_FF + lo:D_FF + hi])
            act = (gate * _sigmoid(gate) * up).astype(_MXU)
            part = _dot(act, wout_ref[lo:hi, :])
            acc = part if acc is None else acc + part
        y = x_ref[rows, :] + (0.5 * g) * acc
        if final:
            y = _rms(y, rest[0][...])
        o_ref[rows, :] = y


def _ffn(x, layer, mod, gain, w_in, w_out, piece, final_gain=None):
    batch, seq, _ = x.shape
    final = final_gain is not None
    in_specs = [
        pl.BlockSpec((None, FFN_ROWS, D_MODEL), lambda b, i: (b, i, 0)),
        _mod_spec(layer),
        _resident((1, D_MODEL), layer),
        _resident((D_MODEL, 2 * D_FF), layer),
        _resident((D_FF, D_MODEL), layer),
    ]
    args = [x, mod, gain, w_in, w_out]
    if final:
        in_specs.append(_resident((1, D_MODEL)))
        args.append(final_gain.reshape(1, D_MODEL))
    return pl.pallas_call(
        functools.partial(_ffn_kernel, piece=piece, final=final),
        out_shape=jax.ShapeDtypeStruct(x.shape, _F32),
        grid=(batch, seq // FFN_ROWS),
        in_specs=in_specs,
        out_specs=pl.BlockSpec((None, FFN_ROWS, D_MODEL), lambda b, i: (b, i, 0)),
        compiler_params=_params(("arbitrary", "arbitrary")),
        name="ffn_final" if final else "ffn",
    )(*args)


def _inproj_kernel(x_ref, mod_ref, gain_ref, w_ref, wg_ref, bg_ref,
                   qa_ref, ka_ref, kas_ref, vat_ref,
                   qb_ref, kb_ref, vb_ref, g_ref, sr_ref, sga_ref, sgb_ref):
    subs = [slice(s * IN_SUB, (s + 1) * IN_SUB) for s in range(ROW_TILE // IN_SUB)]
    hbs = [_modulated_norm(x_ref[rows, :], mod_ref, gain_ref, 1).astype(_MXU) for rows in subs]
    for rows, hb in zip(subs, hbs):
        def proj(lo, width):
            return _dot(hb, w_ref[:, lo:lo + width])

        lr = proj(_C_LR, LANES).astype(_MXU)
        z = _dot(lr, wg_ref[...]) + bg_ref[...]
        log_sig = jnp.minimum(z, 0.0) - jnp.log1p(jnp.exp(-jnp.abs(z)))
        g_ref[rows, :] = log_sig / GLA_GATE_NORMALIZER
        r = proj(_C_R, GLA_V_W)
        sr_ref[rows, :] = (r * _sigmoid(r)).astype(sr_ref.dtype)
        sga_ref[rows, :] = _sigmoid(proj(_C_GA, D_MODEL)).astype(sga_ref.dtype)
        sgb_ref[rows, :] = _sigmoid(proj(_C_GB, D_MODEL)).astype(sgb_ref.dtype)
        kva = proj(_C_KA, 2 * ATT_KV_W)
        ka = kva[:, :ATT_KV_W] * (HEAD_DIM ** -0.5)
        ka_ref[rows, :] = ka.astype(ka_ref.dtype)
        kas_ref[rows, :] = pltpu.roll(ka, HEAD_DIM, axis=1).astype(kas_ref.dtype)
        vat_ref[:, rows] = kva[:, ATT_KV_W:].T.astype(vat_ref.dtype)
        qa_ref[rows, :] = proj(_C_QA, ATT_Q_W).astype(qa_ref.dtype)
        qb_ref[rows, :] = proj(_C_QB, GLA_K_W)
        kb_ref[rows, :] = proj(_C_KB, GLA_K_W)
        vb_ref[rows, :] = proj(_C_VB, GLA_V_W).astype(vb_ref.dtype)


def _inproj(x, layer, mod, gain, w_all, w_gate_p, b_gate):
    batch, seq, _ = x.shape

    def tok(width):
        return pl.BlockSpec((None, ROW_TILE, width), lambda b, i: (b, i, 0))

    def out(width, dtype):
        return jax.ShapeDtypeStruct((batch, seq, width), dtype)

    outs = [
        (ATT_Q_W, _MXU), (ATT_KV_W, _MXU), (ATT_KV_W, _MXU), None,
        (GLA_K_W, _F32), (GLA_K_W, _F32), (GLA_V_W, _MXU), (GLA_K_W, _F32),
        (GLA_V_W, _MXU), (D_MODEL, _MXU), (D_MODEL, _MXU),
    ]
    vat_shape = jax.ShapeDtypeStruct((batch, ATT_KV_W, seq), _MXU)
    vat_spec = pl.BlockSpec((None, ATT_KV_W, ROW_TILE), lambda b, i: (b, 0, i))
    return pl.pallas_call(
        _inproj_kernel,
        out_shape=[out(*o) if o else vat_shape for o in outs],
        grid=(batch, seq // ROW_TILE),
        in_specs=[
            tok(D_MODEL),
            _mod_spec(layer),
            _resident((1, D_MODEL), layer),
            _resident((D_MODEL, _C_END), layer),
            _resident((LANES, GLA_K_W), layer),
            _resident((1, GLA_K_W), layer),
        ],
        out_specs=[tok(o[0]) if o else vat_spec for o in outs],
        compiler_params=_params(("arbitrary", "arbitrary")),
        name="mix_inproj",
    )(x, mod, gain, w_all, w_gate_p, b_gate)


def _swa_kernel(sinks_ref, q_ref, kp_ref, kc_ref, ksp_ref, ksc_ref, vtp_ref, vtc_ref, gate_ref, o_ref,
                *, sink_base):
    n = pl.program_id(1)
    grp = N_Q_HEADS // N_KV_HEADS
    pairs = grp // 2
    band = 2 * SEQ_BLOCK

    k_all = jnp.concatenate([kp_ref[...], kc_ref[...]], axis=0)
    ks_all = jnp.concatenate([ksp_ref[...], ksc_ref[...]], axis=0)
    vt_all = jnp.concatenate([vtp_ref[...], vtc_ref[...]], axis=1)
    low = lax.broadcasted_iota(jnp.int32, (band, LANES), 1) < HEAD_DIM
    zero = jnp.zeros((band, LANES), k_all.dtype)
    ones = jnp.ones((HEAD_DIM, band), vt_all.dtype)

    kj = lax.broadcasted_iota(jnp.int32, (band, SEQ_BLOCK), 0)
    qi = lax.broadcasted_iota(jnp.int32, (band, SEQ_BLOCK), 1)
    in_window = (kj > qi) & (kj <= qi + WINDOW)

    def scores(j, kvh, parity):
        r0 = j * SEQ_BLOCK
        valid = in_window if j > 0 else in_window & ((kj >= SEQ_BLOCK) | (n > 0))
        bias = jnp.concatenate([jnp.where(valid, 0.0, -jnp.inf)] * pairs, axis=1)
        nat, swp = (k_all, ks_all) if kvh == 0 else (ks_all, k_all)
        if parity == 0:
            k_m = jnp.where(low, nat[r0:r0 + band], zero)
        else:
            k_m = jnp.where(low, zero, swp[r0:r0 + band])
        q = jnp.concatenate(
            [q_ref[r0:r0 + SEQ_BLOCK, (kvh * pairs + p) * LANES:(kvh * pairs + p + 1) * LANES]
             for p in range(pairs)], axis=0)
        return _dot_nt(k_m, q) + bias

    def attend(j, kvh, parity, s):
        r0 = j * SEQ_BLOCK
        v_h = vt_all[kvh * HEAD_DIM:(kvh + 1) * HEAD_DIM, r0:r0 + band]
        vt_m = jnp.concatenate([v_h, ones] if parity == 0 else [ones, v_h], axis=0)
        sink = jnp.concatenate(
            [jnp.full((1, SEQ_BLOCK), sinks_ref[sink_base + kvh * grp + 2 * p + parity], _F32)
             for p in range(pairs)], axis=1)
        m = jnp.maximum(jnp.max(s, axis=0, keepdims=True), sink)
        pv = _dot(vt_m, jnp.exp(s - m).astype(_MXU))
        if parity == 0:
            num, den = pv[:HEAD_DIM], pv[HEAD_DIM:HEAD_DIM + 1]
        else:
            num, den = pv[HEAD_DIM:], pv[:1]
        return num * (1.0 / (den + jnp.exp(sink - m)))

    groups = [(j, kvh, parity) for j in range(SWA_BLOCKS) for kvh in range(N_KV_HEADS) for parity in range(2)]
    s_next = scores(*groups[0])
    halves = []
    for i, (j, kvh, parity) in enumerate(groups):
        s = s_next
        if i + 1 < len(groups):
            s_next = scores(*groups[i + 1])
        halves.append(attend(j, kvh, parity, s))
        if parity == 1:
            out_t = jnp.concatenate(halves, axis=0)
            halves = []
            for p in range(pairs):
                rows = slice(j * SEQ_BLOCK, (j + 1) * SEQ_BLOCK)
                cols = slice((kvh * pairs + p) * LANES, (kvh * pairs + p + 1) * LANES)
                y = out_t[:, p * SEQ_BLOCK:(p + 1) * SEQ_BLOCK].T
                o_ref[rows, cols] = (gate_ref[rows, cols].astype(_F32) * y).astype(o_ref.dtype)


def _swa(layer, sinks, qa, ka, kas, vat, gate):
    batch, seq, _ = qa.shape
    rows = SWA_BLOCKS * SEQ_BLOCK

    def cur(width):
        return pl.BlockSpec((None, rows, width), lambda b, n: (b, n, 0))

    def prev(width):
        return pl.BlockSpec((None, SEQ_BLOCK, width),
                            lambda b, n: (b, jnp.maximum(n * SWA_BLOCKS - 1, 0), 0))

    vt_cur = pl.BlockSpec((None, ATT_KV_W, rows), lambda b, n: (b, 0, n))
    vt_prev = pl.BlockSpec((None, ATT_KV_W, SEQ_BLOCK),
                           lambda b, n: (b, 0, jnp.maximum(n * SWA_BLOCKS - 1, 0)))
    return pl.pallas_call(
        functools.partial(_swa_kernel, sink_base=layer * N_Q_HEADS),
        out_shape=jax.ShapeDtypeStruct((batch, seq, ATT_Q_W), _MXU),
        grid=(batch, seq // rows),
        in_specs=[
            pl.BlockSpec(memory_space=pltpu.SMEM),
            cur(ATT_Q_W),
            prev(ATT_KV_W), cur(ATT_KV_W), prev(ATT_KV_W), cur(ATT_KV_W),
            vt_prev, vt_cur,
            cur(ATT_Q_W),
        ],
        out_specs=cur(ATT_Q_W),
        compiler_params=_params(("arbitrary", "arbitrary")),
        name="swa",
    )(sinks, qa, ka, ka, kas, kas, vat, vat, gate)


def _rows_bcast(x, starts, reps):
    return jnp.concatenate(
        [jnp.broadcast_to(x[s:s + 1, :], (reps, x.shape[1])) for s in starts], axis=0)


def _gla_kernel(q_ref, k_ref, v_ref, g_ref, gain_ref, sr_ref, sgb_ref, za_ref, x_ref, mod_ref, w_ref,
                o_ref, state_ref, merged_ref):
    n = pl.program_id(1)
    C = SEQ_BLOCK

    @pl.when(n == 0)
    def _():
        state_ref[...] = jnp.zeros_like(state_ref)

    ri = lax.broadcasted_iota(jnp.int32, (C, C), 0)
    ci = lax.broadcasted_iota(jnp.int32, (C, C), 1)
    tri = jnp.where(ci <= ri, 1.0, 0.0).astype(_MXU)
    diag_mask = (ri // GLA_SUB == ci // GLA_SUB) & (ci <= ri)
    levels = []
    b = GLA_SUB
    while b < C:
        pair_mask = (ri // (2 * b) == ci // (2 * b)) & ((ri // b) % 2 == 1) & ((ci // b) % 2 == 0)
        right = (lax.broadcasted_iota(jnp.int32, (C, GLA_DK), 0) // b) % 2 == 1
        levels.append((b, pair_mask, right))
        b *= 2

    def cum_decay(c):
        g = g_ref[c * C:(c + 1) * C, :] * LOG2_E
        g1 = g.astype(_MXU)
        r1 = g - g1.astype(_F32)
        g2 = r1.astype(_MXU)
        g3 = (r1 - g2.astype(_F32)).astype(_MXU)
        return _dot(tri, g1) + _dot(tri, g2) + _dot(tri, g3)

    def intra(c, h, G_all):
        rows = slice(c * C, (c + 1) * C)
        ks = slice(h * GLA_DK, (h + 1) * GLA_DK)
        G = G_all[:, ks]
        q = q_ref[rows, ks] * (GLA_DK ** -0.5)
        k = k_ref[rows, ks]
        G_last = G[C - 1:C, :]
        G_first = _rows_bcast(G, range(0, C, GLA_SUB), GLA_SUB)
        a = jnp.where(
            diag_mask,
            _dot_nt((q * jnp.exp2(G - G_first)).astype(_MXU), (k * jnp.exp2(G_first - G)).astype(_MXU)),
            0.0)
        for b, pair_mask, right in levels:
            ref = _rows_bcast(G, range(b - 1, C, 2 * b), 2 * b)
            e = jnp.exp2(jnp.where(right, G - ref, ref - G))
            a = a + jnp.where(pair_mask, _dot_nt((q * e).astype(_MXU), (k * e).astype(_MXU)), 0.0)
        qg = (q * jnp.exp2(G)).astype(_MXU)
        kg = (k * jnp.exp2(G_last - G)).astype(_MXU)
        return a.astype(_MXU), qg, kg, jnp.exp2(G_last)

    def inter(c, h, a, qg, kg, dec):
        rows = slice(c * C, (c + 1) * C)
        vs = slice(h * GLA_DV, (h + 1) * GLA_DV)
        v = v_ref[rows, vs]
        state = state_ref[h]
        lhs = jnp.concatenate([a, qg], axis=1)
        rhs = jnp.concatenate([v, state.astype(_MXU)], axis=0)
        o = _dot(lhs, rhs)
        upd = _dot_tn(kg, v)
        dec_t = jnp.transpose(jnp.broadcast_to(dec, (C, GLA_DK)))
        state_ref[h] = state * jnp.concatenate([dec_t] * (GLA_DV // GLA_DK), axis=1) + upd
        yb = _rms(o, gain_ref[...]) * sr_ref[rows, vs].astype(_F32)
        merged = za_ref[rows, vs].astype(_F32) + sgb_ref[rows, vs].astype(_F32) * yb
        merged_ref[rows, vs] = merged.astype(_MXU)

    units = [(c, h) for c in range(GLA_CHUNKS) for h in range(GLA_HEADS)]
    G_all = cum_decay(0)
    nxt = intra(0, 0, G_all)
    for i, (c, h) in enumerate(units):
        cur = nxt
        if i + 1 < len(units):
            c2, h2 = units[i + 1]
            if c2 != c:
                G_all = cum_decay(c2)
            nxt = intra(c2, h2, G_all)
        inter(c, h, *cur)
        if h == GLA_HEADS - 1 and (c + 1) % GLA_PROJ_CHUNKS == 0:
            rows = slice((c + 1 - GLA_PROJ_CHUNKS) * C, (c + 1) * C)
            out = _dot(merged_ref[rows, :], w_ref[...])
            o_ref[rows, :] = x_ref[rows, :] + mod_ref[5:6, :] * out


def _gla_merge(x, layer, mod, qb, kb, vb, g, gain, sr, sgb, za, w_out):
    batch, seq, _ = qb.shape
    rows = GLA_CHUNKS * SEQ_BLOCK

    def tok(width):
        return pl.BlockSpec((None, rows, width), lambda b, n: (b, n, 0))

    return pl.pallas_call(
        _gla_kernel,
        out_shape=jax.ShapeDtypeStruct(x.shape, _F32),
        grid=(batch, seq // rows),
        in_specs=[tok(GLA_K_W), tok(GLA_K_W), tok(GLA_V_W), tok(GLA_K_W), _resident((1, GLA_DV), layer),
                  tok(GLA_V_W), tok(D_MODEL), tok(D_MODEL), tok(D_MODEL),
                  _mod_spec(layer),
                  _resident((D_MODEL, D_MODEL), layer)],
        out_specs=tok(D_MODEL),
        scratch_shapes=[pltpu.VMEM((GLA_HEADS, GLA_DK, GLA_DV), _F32),
                        pltpu.VMEM((rows, D_MODEL), _MXU)],
        compiler_params=_params(("arbitrary", "arbitrary")),
        name="gla_merge",
    )(qb, kb, vb, g, gain, sr, sgb, za, x, mod, w_out)


def _arrange_mix_w_in(w):
    sizes = (ATT_Q_W, ATT_KV_W, ATT_KV_W, GLA_K_W, GLA_K_W, GLA_V_W, GLA_RANK, GLA_V_W, D_MODEL, D_MODEL)
    offs = [0]
    for s in sizes:
        offs.append(offs[-1] + s)
    pieces = [w[..., offs[i]:offs[i + 1]] for i in range(len(sizes))]
    lr = jnp.pad(pieces[6], ((0, 0), (0, 0), (0, LANES - GLA_RANK)))
    return jnp.concatenate(pieces[:6] + pieces[7:] + [lr], axis=-1).astype(_MXU)


def kernel(x, c, ada_w, ada_b, norm1, ffn1_w_in, ffn1_w_out, norm_mix, mix_w_in, attn_sinks,
           gla_w_gate, gla_b_gate, gla_out_norm, mix_w_out, norm2, ffn2_w_in, ffn2_w_out, final_norm):
    depth = ada_w.shape[0]
    assert x.shape[1] % FFN_ROWS == 0 and x.shape[2] == D_MODEL
    ffn_w = [(w_in.astype(_MXU), w_out.astype(_MXU))
             for w_in, w_out in ((ffn1_w_in, ffn1_w_out), (ffn2_w_in, ffn2_w_out))]
    ffn_gain = [n.reshape(depth, 1, D_MODEL) for n in (norm1, norm2)]
    mix_gain = norm_mix.reshape(depth, 1, D_MODEL)
    w_all = _arrange_mix_w_in(mix_w_in)
    w_gate_p = jnp.pad(gla_w_gate, ((0, 0), (0, LANES - GLA_RANK), (0, 0))).astype(_MXU)
    b_gate = gla_b_gate.reshape(depth, 1, GLA_K_W)
    gla_gain = gla_out_norm.reshape(depth, 1, GLA_DV)
    w_out = mix_w_out.astype(_MXU)
    sinks = attn_sinks.reshape(depth * N_Q_HEADS)

    mod = _modulation(c, ada_w, ada_b)
    for l in range(depth):
        x = _ffn(x, l, mod, ffn_gain[0], *ffn_w[0], 0)
        qa, ka, kas, vat, qb, kb, vb, g, sr, sga, sgb = _inproj(x, l, mod, mix_gain, w_all, w_gate_p, b_gate)
        za = _swa(l, sinks, qa, ka, kas, vat, sga)
        x = _gla_merge(x, l, mod, qb, kb, vb, g, gla_gain, sr, sgb, za, w_out)
        x = _ffn(x, l, mod, ffn_gain[1], *ffn_w[1], 2,
                 final_gain=final_norm if l == depth - 1 else None)
    return x
```

```python
import functools

import jax
import jax.numpy as jnp
from jax import lax
from jax.experimental import pallas as pl
from jax.experimental.pallas import tpu as pltpu

D_MODEL = 1024
N_Q_HEADS = 16
N_KV_HEADS = 2
HEAD_DIM = 64
WINDOW = 128
GLA_HEADS = 4
GLA_DK = 128
GLA_DV = 256
GLA_RANK = 16
GLA_GATE_NORMALIZER = 16.0
GLA_SUB = 16
D_FF = 2816
N_MOD = 9
EPS = 1e-6
LOG2_E = 1.4426950408889634

ATT_Q_W = N_Q_HEADS * HEAD_DIM
ATT_KV_W = N_KV_HEADS * HEAD_DIM
GLA_K_W = GLA_HEADS * GLA_DK
GLA_V_W = GLA_HEADS * GLA_DV

LANES = 128
MXU_WIDTH = 256
SEQ_BLOCK = 128
MIX_ROWS = 512
MIX_BLOCKS = MIX_ROWS // SEQ_BLOCK
GLA_PROJ_CHUNKS = 2
SUB_TILE = 256
PROJ_COLS = 2 * MXU_WIDTH
FFN_ROWS = 1024
FF_SPLITS = (0, 6 * MXU_WIDTH, D_FF)
VMEM_LIMIT = 56 * 1024 * 1024

_MXU = jnp.bfloat16
_F32 = jnp.float32

_A_Q = 0
_A_KV = _A_Q + ATT_Q_W
_A_G = _A_KV + 2 * ATT_KV_W
_A_END = _A_G + D_MODEL
_B_Q = 0
_B_K = _B_Q + GLA_K_W
_B_V = _B_K + GLA_K_W
_B_R = _B_V + GLA_V_W
_B_G = _B_R + GLA_V_W
_B_LR = _B_G + D_MODEL
_B_END = _B_LR + LANES


def _dot(a, b):
    return jnp.dot(a, b, preferred_element_type=_F32)


def _dot_nt(a, b):
    return lax.dot_general(a, b, (((1,), (1,)), ((), ())), preferred_element_type=_F32)


def _dot_tn(a, b):
    return lax.dot_general(a, b, (((0,), (0,)), ((), ())), preferred_element_type=_F32)


def _sigmoid(x):
    return 1.0 / (1.0 + jnp.exp(-x))


def _rms(x, gain):
    ms = jnp.mean(x * x, axis=-1, keepdims=True)
    return x * lax.rsqrt(ms + EPS) * gain


def _params(sem):
    return pltpu.CompilerParams(dimension_semantics=sem, vmem_limit_bytes=VMEM_LIMIT)


def _resident(shape, layer=None):
    if layer is None:
        return pl.BlockSpec(shape, lambda *_: (0,) * len(shape), pipeline_mode=pl.Buffered(1))
    return pl.BlockSpec((None,) + tuple(shape), lambda *_: (layer,) + (0,) * len(shape),
                        pipeline_mode=pl.Buffered(1))


def _mod_spec(layer):
    return pl.BlockSpec((None, None, N_MOD, D_MODEL), lambda b, i: (layer, b, 0, 0))


def _sub_tiles(rows):
    return [slice(s * SUB_TILE, (s + 1) * SUB_TILE) for s in range(rows // SUB_TILE)]


def _mod_kernel(c_ref, w_ref, b_ref, o_ref):
    c = c_ref[...]
    cond = c * _sigmoid(c)
    o_ref[...] = _dot(cond.astype(_MXU), w_ref[...].astype(_MXU)) + b_ref[...]


def _modulation(c, ada_w, ada_b):
    depth = ada_w.shape[0]
    batch = c.shape[0]
    out = pl.pallas_call(
        _mod_kernel,
        out_shape=jax.ShapeDtypeStruct((depth, N_MOD, batch, D_MODEL), _F32),
        grid=(depth, N_MOD),
        in_specs=[
            pl.BlockSpec((batch, D_MODEL), lambda l, j: (0, 0)),
            pl.BlockSpec((None, D_MODEL, D_MODEL), lambda l, j: (l, 0, j)),
            pl.BlockSpec((None, None, 1, D_MODEL), lambda l, j: (l, j, 0, 0)),
        ],
        out_specs=pl.BlockSpec((None, None, batch, D_MODEL), lambda l, j: (l, j, 0, 0)),
        compiler_params=_params(("arbitrary", "arbitrary")),
        name="adaln_mod",
    )(c, ada_w, ada_b.reshape(depth, N_MOD, 1, D_MODEL))
    return jnp.transpose(out, (0, 2, 1, 3))


def _modulated_norm(x, mod_ref, gain_ref, piece):
    sh = mod_ref[3 * piece:3 * piece + 1, :]
    sc = mod_ref[3 * piece + 1:3 * piece + 2, :]
    return _rms(x, gain_ref[...]) * (1.0 + sc) + sh


def _ffn_kernel(x_ref, mod_ref, gain_ref, win_ref, wout_ref, *rest, piece, final):
    o_ref = rest[-1]
    g = mod_ref[3 * piece + 2:3 * piece + 3, :]
    subs = _sub_tiles(FFN_ROWS)
    hbs = [_modulated_norm(x_ref[rows, :], mod_ref, gain_ref, piece).astype(_MXU) for rows in subs]
    for rows, hb in zip(subs, hbs):
        acc = None
        for lo, hi in zip(FF_SPLITS[:-1], FF_SPLITS[1:]):
            gate = _dot(hb, win_ref[:, lo:hi])
            up = _dot(hb, win_ref[:, D_FF + lo:D_FF + hi])
            act = (gate * _sigmoid(gate) * up).astype(_MXU)
            part = _dot(act, wout_ref[lo:hi, :])
            acc = part if acc is None else acc + part
        y = x_ref[rows, :] + (0.5 * g) * acc
        if final:
            y = _rms(y, rest[0][...])
        o_ref[rows, :] = y


def _ffn(x, layer, mod, gain, w_in, w_out, piece, final_gain=None):
    batch, seq, _ = x.shape
    final = final_gain is not None
    in_specs = [
        pl.BlockSpec((None, FFN_ROWS, D_MODEL), lambda b, i: (b, i, 0)),
        _mod_spec(layer),
        _resident((1, D_MODEL), layer),
        _resident((D_MODEL, 2 * D_FF), layer),
        _resident((D_FF, D_MODEL), layer),
    ]
    args = [x, mod, gain, w_in, w_out]
    if final:
        in_specs.append(_resident((1, D_MODEL)))
        args.append(final_gain.reshape(1, D_MODEL))
    return pl.pallas_call(
        functools.partial(_ffn_kernel, piece=piece, final=final),
        out_shape=jax.ShapeDtypeStruct(x.shape, _F32),
        grid=(batch, seq // FFN_ROWS),
        in_specs=in_specs,
        out_specs=pl.BlockSpec((None, FFN_ROWS, D_MODEL), lambda b, i: (b, i, 0)),
        compiler_params=_params(("arbitrary", "arbitrary")),
        name="ffn_final" if final else "ffn",
    )(*args)


def _next_tile(b, i, batch, tiles):
    flat = jnp.minimum(b * tiles + i + 1, batch * tiles - 1)
    return flat // tiles, flat % tiles


def _next_specs(layer, batch, tiles):
    def x_map(b, i):
        nb, ni = _next_tile(b, i, batch, tiles)
        return nb, ni, 0

    def mod_map(b, i):
        nb, _ = _next_tile(b, i, batch, tiles)
        return layer, nb, 0, 0

    return (pl.BlockSpec((None, MIX_ROWS, D_MODEL), x_map),
            pl.BlockSpec((None, None, N_MOD, D_MODEL), mod_map))


class _Tasks:
    def __init__(self, pieces, steps):
        self._pieces = iter(pieces)
        self._per_step = -(-len(pieces) // steps)

    def run(self, k=None):
        for _ in range(self._per_step if k is None else k):
            piece = next(self._pieces, None)
            if piece is not None:
                piece()

    def finish(self):
        for piece in self._pieces:
            piece()


def _pipelined(first_pieces, next_pieces, work, steps):
    b = pl.program_id(0)
    i = pl.program_id(1)

    @pl.when((b == 0) & (i == 0))
    def _():
        for piece in first_pieces(0):
            piece()

    for parity in (0, 1):
        @pl.when(i % 2 == parity)
        def _(parity=parity):
            tasks = _Tasks(next_pieces(1 - parity), steps)
            work(parity, tasks)
            tasks.finish()


def _col_chunks(lo, hi):
    return [(c, min(c + PROJ_COLS, hi)) for c in range(lo, hi, PROJ_COLS)]


def _swa_project_pieces(x_ref, mod_ref, gain_ref, w_ref, slot):
    q_s, k_s, ks_s, vt_s, gate_s = slot
    pieces = []
    for rows in _sub_tiles(MIX_ROWS):
        hb = []

        def norm(rows=rows, hb=hb):
            hb.append(_modulated_norm(x_ref[rows, :], mod_ref, gain_ref, 1).astype(_MXU))

        def kv_proj(rows=rows, hb=hb):
            kv = _dot(hb[0], w_ref[:, _A_KV:_A_G])
            k = kv[:, :ATT_KV_W] * (HEAD_DIM ** -0.5)
            k_s[rows, :] = k.astype(k_s.dtype)
            ks_s[rows, :] = pltpu.roll(k, HEAD_DIM, axis=1).astype(ks_s.dtype)
            vt_s[:, rows] = kv[:, ATT_KV_W:].T.astype(vt_s.dtype)

        def q_proj(lo, hi, rows=rows, hb=hb):
            q_s[rows, lo - _A_Q:hi - _A_Q] = _dot(hb[0], w_ref[:, lo:hi]).astype(q_s.dtype)

        def gate_proj(lo, hi, rows=rows, hb=hb):
            gate_s[rows, lo - _A_G:hi - _A_G] = _sigmoid(_dot(hb[0], w_ref[:, lo:hi])).astype(gate_s.dtype)

        pieces += [norm, kv_proj]
        pieces += [functools.partial(q_proj, lo, hi) for lo, hi in _col_chunks(_A_Q, _A_KV)]
        pieces += [functools.partial(gate_proj, lo, hi) for lo, hi in _col_chunks(_A_G, _A_END)]
    return pieces


SWA_STEPS = MIX_BLOCKS * N_KV_HEADS * 2


def _swa_attend(sinks_ref, slot, carry, o_ref, sink_base, tasks):
    q_s, k_s, ks_s, vt_s, gate_s = slot
    ck, cks, cvt = carry
    n = pl.program_id(1)
    grp = N_Q_HEADS // N_KV_HEADS
    pairs = grp // 2
    band = 2 * SEQ_BLOCK

    k_all = jnp.concatenate([ck[...], k_s[...]], axis=0)
    ks_all = jnp.concatenate([cks[...], ks_s[...]], axis=0)
    vt_all = jnp.concatenate([cvt[...], vt_s[...]], axis=1)
    low = lax.broadcasted_iota(jnp.int32, (band, LANES), 1) < HEAD_DIM
    zero = jnp.zeros((band, LANES), k_all.dtype)
    ones = jnp.ones((HEAD_DIM, band), vt_all.dtype)

    kj = lax.broadcasted_iota(jnp.int32, (band, SEQ_BLOCK), 0)
    qi = lax.broadcasted_iota(jnp.int32, (band, SEQ_BLOCK), 1)
    in_window = (kj > qi) & (kj <= qi + WINDOW)

    def scores(j, kvh, parity):
        r0 = j * SEQ_BLOCK
        valid = in_window if j > 0 else in_window & ((kj >= SEQ_BLOCK) | (n > 0))
        bias = jnp.concatenate([jnp.where(valid, 0.0, -jnp.inf)] * pairs, axis=1)
        nat, swp = (k_all, ks_all) if kvh == 0 else (ks_all, k_all)
        if parity == 0:
            k_m = jnp.where(low, nat[r0:r0 + band], zero)
        else:
            k_m = jnp.where(low, zero, swp[r0:r0 + band])
        q = jnp.concatenate(
            [q_s[r0:r0 + SEQ_BLOCK, (kvh * pairs + p) * LANES:(kvh * pairs + p + 1) * LANES]
             for p in range(pairs)], axis=0)
        return _dot_nt(k_m, q) + bias

    def attend(j, kvh, parity, s):
        r0 = j * SEQ_BLOCK
        v_h = vt_all[kvh * HEAD_DIM:(kvh + 1) * HEAD_DIM, r0:r0 + band]
        vt_m = jnp.concatenate([v_h, ones] if parity == 0 else [ones, v_h], axis=0)
        sink = jnp.concatenate(
            [jnp.full((1, SEQ_BLOCK), sinks_ref[sink_base + kvh * grp + 2 * p + parity], _F32)
             for p in range(pairs)], axis=1)
        m = jnp.maximum(jnp.max(s, axis=0, keepdims=True), sink)
        pv = _dot(vt_m, jnp.exp(s - m).astype(_MXU))
        if parity == 0:
            num, den = pv[:HEAD_DIM], pv[HEAD_DIM:HEAD_DIM + 1]
        else:
            num, den = pv[HEAD_DIM:], pv[:1]
        return num * (1.0 / (den + jnp.exp(sink - m)))

    groups = [(j, kvh, parity) for j in range(MIX_BLOCKS) for kvh in range(N_KV_HEADS) for parity in range(2)]
    s_next = scores(*groups[0])
    halves = []
    for idx, (j, kvh, parity) in enumerate(groups):
        s = s_next
        if idx + 1 < len(groups):
            s_next = scores(*groups[idx + 1])
        halves.append(attend(j, kvh, parity, s))
        tasks.run()
        if parity == 1:
            out_t = jnp.concatenate(halves, axis=0)
            halves = []
            for p in range(pairs):
                rows = slice(j * SEQ_BLOCK, (j + 1) * SEQ_BLOCK)
                cols = slice((kvh * pairs + p) * LANES, (kvh * pairs + p + 1) * LANES)
                y = out_t[:, p * SEQ_BLOCK:(p + 1) * SEQ_BLOCK].T
                o_ref[rows, cols] = (gate_s[rows, cols].astype(_F32) * y).astype(o_ref.dtype)

    last = slice(MIX_ROWS - SEQ_BLOCK, MIX_ROWS)
    ck[...] = k_s[last, :]
    cks[...] = ks_s[last, :]
    cvt[...] = vt_s[:, last]


def _swa_kernel(sinks_ref, x0_ref, mod0_ref, xn_ref, modn_ref, gain_ref, w_ref, o_ref, *scratch, sink_base):
    slots = (scratch[0:5], scratch[5:10])
    carry = scratch[10:13]

    def zero_carry():
        for c in carry:
            c[...] = jnp.zeros_like(c)

    _pipelined(lambda s: [zero_carry] + _swa_project_pieces(x0_ref, mod0_ref, gain_ref, w_ref, slots[s]),
               lambda s: _swa_project_pieces(xn_ref, modn_ref, gain_ref, w_ref, slots[s]),
               lambda s, tasks: _swa_attend(sinks_ref, slots[s], carry, o_ref, sink_base, tasks),
               SWA_STEPS)


def _swa(x, layer, mod, gain, w_a, sinks):
    batch, seq, _ = x.shape
    tiles = seq // MIX_ROWS
    x_next, mod_next = _next_specs(layer, batch, tiles)
    slot = [pltpu.VMEM((MIX_ROWS, ATT_Q_W), _MXU), pltpu.VMEM((MIX_ROWS, ATT_KV_W), _MXU),
            pltpu.VMEM((MIX_ROWS, ATT_KV_W), _MXU), pltpu.VMEM((ATT_KV_W, MIX_ROWS), _MXU),
            pltpu.VMEM((MIX_ROWS, D_MODEL), _MXU)]
    carry = [pltpu.VMEM((SEQ_BLOCK, ATT_KV_W), _MXU), pltpu.VMEM((SEQ_BLOCK, ATT_KV_W), _MXU),
             pltpu.VMEM((ATT_KV_W, SEQ_BLOCK), _MXU)]
    return pl.pallas_call(
        functools.partial(_swa_kernel, sink_base=layer * N_Q_HEADS),
        out_shape=jax.ShapeDtypeStruct((batch, seq, D_MODEL), _MXU),
        grid=(batch, tiles),
        in_specs=[
            pl.BlockSpec(memory_space=pltpu.SMEM),
            pl.BlockSpec((None, MIX_ROWS, D_MODEL), lambda b, i: (0, 0, 0)),
            pl.BlockSpec((None, None, N_MOD, D_MODEL), lambda b, i: (layer, 0, 0, 0)),
            x_next, mod_next,
            _resident((1, D_MODEL), layer),
            _resident((D_MODEL, _A_END), layer),
        ],
        out_specs=pl.BlockSpec((None, MIX_ROWS, D_MODEL), lambda b, i: (b, i, 0)),
        scratch_shapes=slot + slot + carry,
        compiler_params=_params(("arbitrary", "arbitrary")),
        name="swa",
    )(sinks, x, mod, x, mod, gain, w_a)


def _gla_project_pieces(x_ref, mod_ref, gain_ref, w_ref, wg_ref, bg_ref, slot):
    q_s, k_s, g_s, v_s, sr_s, sgb_s = slot
    pieces = []
    for rows in _sub_tiles(MIX_ROWS):
        hb = []

        def norm(rows=rows, hb=hb):
            hb.append(_modulated_norm(x_ref[rows, :], mod_ref, gain_ref, 1).astype(_MXU))

        def decay_gate(rows=rows, hb=hb):
            lr = _dot(hb[0], w_ref[:, _B_LR:_B_END]).astype(_MXU)
            z = _dot(lr, wg_ref[...]) + bg_ref[...]
            log_sig = jnp.minimum(z, 0.0) - jnp.log1p(jnp.exp(-jnp.abs(z)))
            g_s[rows, :] = log_sig / GLA_GATE_NORMALIZER

        def proj(dst, base, act, lo, hi, rows=rows, hb=hb):
            dst[rows, lo - base:hi - base] = act(_dot(hb[0], w_ref[:, lo:hi])).astype(dst.dtype)

        pieces += [norm, decay_gate]
        for dst, lo, hi, act in ((sr_s, _B_R, _B_G, lambda r: r * _sigmoid(r)),
                                 (sgb_s, _B_G, _B_LR, _sigmoid),
                                 (q_s, _B_Q, _B_K, lambda t: t),
                                 (k_s, _B_K, _B_V, lambda t: t),
                                 (v_s, _B_V, _B_R, lambda t: t)):
            pieces += [functools.partial(proj, dst, lo, act, c0, c1) for c0, c1 in _col_chunks(lo, hi)]
    return pieces


def _rows_bcast(x, starts, reps):
    return jnp.concatenate(
        [jnp.broadcast_to(x[s:s + 1, :], (reps, x.shape[1])) for s in starts], axis=0)


GLA_STEPS = MIX_BLOCKS * GLA_HEADS


def _gla_attend(slot, gain_ref, za_ref, x_ref, mod_ref, w_ref, o_ref, state_ref, merged_ref, tasks):
    q_ref, k_ref, g_ref, v_ref, sr_ref, sgb_ref = slot
    C = SEQ_BLOCK
    ri = lax.broadcasted_iota(jnp.int32, (C, C), 0)
    ci = lax.broadcasted_iota(jnp.int32, (C, C), 1)
    tri = jnp.where(ci <= ri, 1.0, 0.0).astype(_MXU)
    diag_mask = (ri // GLA_SUB == ci // GLA_SUB) & (ci <= ri)
    levels = []
    b = GLA_SUB
    while b < C:
        pair_mask = (ri // (2 * b) == ci // (2 * b)) & ((ri // b) % 2 == 1) & ((ci // b) % 2 == 0)
        right = (lax.broadcasted_iota(jnp.int32, (C, GLA_DK), 0) // b) % 2 == 1
        levels.append((b, pair_mask, right))
        b *= 2

    def cum_decay(c):
        g = g_ref[c * C:(c + 1) * C, :] * LOG2_E
        g1 = g.astype(_MXU)
        r1 = g - g1.astype(_F32)
        g2 = r1.astype(_MXU)
        g3 = (r1 - g2.astype(_F32)).astype(_MXU)
        return _dot(tri, g1) + _dot(tri, g2) + _dot(tri, g3)

    def intra(c, h, G_all):
        rows = slice(c * C, (c + 1) * C)
        ks = slice(h * GLA_DK, (h + 1) * GLA_DK)
        G = G_all[:, ks]
        q = q_ref[rows, ks] * (GLA_DK ** -0.5)
        k = k_ref[rows, ks]
        G_last = G[C - 1:C, :]
        G_first = _rows_bcast(G, range(0, C, GLA_SUB), GLA_SUB)
        a = jnp.where(
            diag_mask,
            _dot_nt((q * jnp.exp2(G - G_first)).astype(_MXU), (k * jnp.exp2(G_first - G)).astype(_MXU)),
            0.0)
        for b, pair_mask, right in levels:
            ref = _rows_bcast(G, range(b - 1, C, 2 * b), 2 * b)
            e = jnp.exp2(jnp.where(right, G - ref, ref - G))
            a = a + jnp.where(pair_mask, _dot_nt((q * e).astype(_MXU), (k * e).astype(_MXU)), 0.0)
        qg = (q * jnp.exp2(G)).astype(_MXU)
        kg = (k * jnp.exp2(G_last - G)).astype(_MXU)
        return a.astype(_MXU), qg, kg, jnp.exp2(G_last)

    def inter(c, h, a, qg, kg, dec):
        rows = slice(c * C, (c + 1) * C)
        vs = slice(h * GLA_DV, (h + 1) * GLA_DV)
        v = v_ref[rows, vs]
        state = state_ref[h]
        lhs = jnp.concatenate([a, qg], axis=1)
        rhs = jnp.concatenate([v, state.astype(_MXU)], axis=0)
        o = _dot(lhs, rhs)
        upd = _dot_tn(kg, v)
        dec_t = jnp.transpose(jnp.broadcast_to(dec, (C, GLA_DK)))
        state_ref[h] = state * jnp.concatenate([dec_t] * (GLA_DV // GLA_DK), axis=1) + upd
        yb = _rms(o, gain_ref[...]) * sr_ref[rows, vs].astype(_F32)
        merged = za_ref[rows, vs].astype(_F32) + sgb_ref[rows, vs].astype(_F32) * yb
        merged_ref[rows, vs] = merged.astype(_MXU)

    units = [(c, h) for c in range(MIX_BLOCKS) for h in range(GLA_HEADS)]
    G_all = cum_decay(0)
    nxt = intra(0, 0, G_all)
    for idx, (c, h) in enumerate(units):
        cur = nxt
        if idx + 1 < len(units):
            c2, h2 = units[idx + 1]
            if c2 != c:
                G_all = cum_decay(c2)
            nxt = intra(c2, h2, G_all)
        inter(c, h, *cur)
        tasks.run()
        if h == GLA_HEADS - 1 and (c + 1) % GLA_PROJ_CHUNKS == 0:
            rows = slice((c + 1 - GLA_PROJ_CHUNKS) * C, (c + 1) * C)
            out = _dot(merged_ref[rows, :], w_ref[...])
            o_ref[rows, :] = x_ref[rows, :] + mod_ref[5:6, :] * out


def _gla_kernel(x_ref, mod_ref, xn_ref, modn_ref, gain_ref, w_ref, wg_ref, bg_ref, gla_gain_ref, za_ref,
                wout_ref, o_ref, *scratch):
    slots = (scratch[0:6], scratch[6:12])
    state_ref, merged_ref = scratch[12:14]

    @pl.when(pl.program_id(1) == 0)
    def _():
        state_ref[...] = jnp.zeros_like(state_ref)

    def pieces(src_ref, src_mod_ref, s):
        return _gla_project_pieces(src_ref, src_mod_ref, gain_ref, w_ref, wg_ref, bg_ref, slots[s])

    _pipelined(lambda s: pieces(x_ref, mod_ref, s),
               lambda s: pieces(xn_ref, modn_ref, s),
               lambda s, tasks: _gla_attend(slots[s], gla_gain_ref, za_ref, x_ref, mod_ref, wout_ref, o_ref,
                                            state_ref, merged_ref, tasks),
               GLA_STEPS)


def _gla_merge(x, layer, mod, gain, w_b, w_gate_p, b_gate, gla_gain, za, w_out):
    batch, seq, _ = x.shape
    tiles = seq // MIX_ROWS
    x_next, mod_next = _next_specs(layer, batch, tiles)

    def tok(width):
        return pl.BlockSpec((None, MIX_ROWS, width), lambda b, i: (b, i, 0))

    slot = [pltpu.VMEM((MIX_ROWS, GLA_K_W), _F32), pltpu.VMEM((MIX_ROWS, GLA_K_W), _F32),
            pltpu.VMEM((MIX_ROWS, GLA_K_W), _F32), pltpu.VMEM((MIX_ROWS, GLA_V_W), _MXU),
            pltpu.VMEM((MIX_ROWS, GLA_V_W), _MXU), pltpu.VMEM((MIX_ROWS, D_MODEL), _MXU)]
    return pl.pallas_call(
        _gla_kernel,
        out_shape=jax.ShapeDtypeStruct(x.shape, _F32),
        grid=(batch, tiles),
        in_specs=[
            tok(D_MODEL), _mod_spec(layer), x_next, mod_next,
            _resident((1, D_MODEL), layer),
            _resident((D_MODEL, _B_END), layer),
            _resident((LANES, GLA_K_W), layer),
            _resident((1, GLA_K_W), layer),
            _resident((1, GLA_DV), layer),
            tok(D_MODEL),
            _resident((D_MODEL, D_MODEL), layer),
        ],
        out_specs=tok(D_MODEL),
        scratch_shapes=slot + slot + [pltpu.VMEM((GLA_HEADS, GLA_DK, GLA_DV), _F32),
                                      pltpu.VMEM((MIX_ROWS, D_MODEL), _MXU)],
        compiler_params=_params(("arbitrary", "arbitrary")),
        name="gla_merge",
    )(x, mod, x, mod, gain, w_b, w_gate_p, b_gate, gla_gain, za, w_out)


def _arrange_mix_w_in(w):
    sizes = (ATT_Q_W, ATT_KV_W, ATT_KV_W, GLA_K_W, GLA_K_W, GLA_V_W, GLA_RANK, GLA_V_W, D_MODEL, D_MODEL)
    offs = [0]
    for s in sizes:
        offs.append(offs[-1] + s)
    qa, ka, va, qb, kb, vb, lr, r, ga, gb = [w[..., offs[i]:offs[i + 1]] for i in range(len(sizes))]
    lr = jnp.pad(lr, ((0, 0), (0, 0), (0, LANES - GLA_RANK)))
    w_a = jnp.concatenate([qa, ka, va, ga], axis=-1).astype(_MXU)
    w_b = jnp.concatenate([qb, kb, vb, r, gb, lr], axis=-1).astype(_MXU)
    return w_a, w_b


def kernel(x, c, ada_w, ada_b, norm1, ffn1_w_in, ffn1_w_out, norm_mix, mix_w_in, attn_sinks,
           gla_w_gate, gla_b_gate, gla_out_norm, mix_w_out, norm2, ffn2_w_in, ffn2_w_out, final_norm):
    depth = ada_w.shape[0]
    seq = x.shape[1]
    assert seq % FFN_ROWS == 0 and (seq // MIX_ROWS) % 2 == 0 and x.shape[2] == D_MODEL
    ffn_w = [(w_in.astype(_MXU), w_out.astype(_MXU))
             for w_in, w_out in ((ffn1_w_in, ffn1_w_out), (ffn2_w_in, ffn2_w_out))]
    ffn_gain = [n.reshape(depth, 1, D_MODEL) for n in (norm1, norm2)]
    mix_gain = norm_mix.reshape(depth, 1, D_MODEL)
    w_a, w_b = _arrange_mix_w_in(mix_w_in)
    w_gate_p = jnp.pad(gla_w_gate, ((0, 0), (0, LANES - GLA_RANK), (0, 0))).astype(_MXU)
    b_gate = gla_b_gate.reshape(depth, 1, GLA_K_W)
    gla_gain = gla_out_norm.reshape(depth, 1, GLA_DV)
    w_out = mix_w_out.astype(_MXU)
    sinks = attn_sinks.reshape(depth * N_Q_HEADS)

    mod = _modulation(c, ada_w, ada_b)
    for l in range(depth):
        x = _ffn(x, l, mod, ffn_gain[0], *ffn_w[0], 0)
        za = _swa(x, l, mod, mix_gain, w_a, sinks)
        x = _gla_merge(x, l, mod, mix_gain, w_b, w_gate_p, b_gate, gla_gain, za, w_out)
        x = _ffn(x, l, mod, ffn_gain[1], *ffn_w[1], 2,
                 final_gain=final_norm if l == depth - 1 else None)
    return x
```

```python
import functools

import jax
import jax.numpy as jnp
from jax import lax
from jax.experimental import pallas as pl
from jax.experimental.pallas import tpu as pltpu

D_MODEL = 1024
N_Q_HEADS = 16
N_KV_HEADS = 2
HEAD_DIM = 64
WINDOW = 128
GLA_HEADS = 4
GLA_DK = 128
GLA_DV = 256
GLA_RANK = 16
GLA_GATE_NORMALIZER = 16.0
GLA_SUB = 16
D_FF = 2816
N_MOD = 9
EPS = 1e-6
LOG2_E = 1.4426950408889634

ATT_Q_W = N_Q_HEADS * HEAD_DIM
ATT_KV_W = N_KV_HEADS * HEAD_DIM
GLA_K_W = GLA_HEADS * GLA_DK
GLA_V_W = GLA_HEADS * GLA_DV

LANES = 128
MXU_WIDTH = 256
SEQ_BLOCK = 128
MIX_ROWS = 512
MIX_BLOCKS = MIX_ROWS // SEQ_BLOCK
GLA_PROJ_CHUNKS = 2
SUB_TILE = 256
MIX_SUB = 256
PROJ_COLS = MXU_WIDTH
FFN_ROWS = 1024
FF_SPLITS = (0, 6 * MXU_WIDTH, D_FF)
VMEM_LIMIT = 56 * 1024 * 1024

_MXU = jnp.bfloat16
_F32 = jnp.float32

_A_Q = 0
_A_KV = _A_Q + ATT_Q_W
_A_G = _A_KV + 2 * ATT_KV_W
_A_END = _A_G + D_MODEL
_B_Q = 0
_B_K = _B_Q + GLA_K_W
_B_V = _B_K + GLA_K_W
_B_R = _B_V + GLA_V_W
_B_G = _B_R + GLA_V_W
_B_LR = _B_G + D_MODEL
_B_END = _B_LR + LANES


def _dot(a, b):
    return jnp.dot(a, b, preferred_element_type=_F32)


def _dot_nt(a, b):
    return lax.dot_general(a, b, (((1,), (1,)), ((), ())), preferred_element_type=_F32)


def _dot_tn(a, b):
    return lax.dot_general(a, b, (((0,), (0,)), ((), ())), preferred_element_type=_F32)


def _sigmoid(x):
    return 1.0 / (1.0 + jnp.exp(-x))


def _rms(x, gain):
    ms = jnp.mean(x * x, axis=-1, keepdims=True)
    return x * lax.rsqrt(ms + EPS) * gain


def _params(sem):
    return pltpu.CompilerParams(dimension_semantics=sem, vmem_limit_bytes=VMEM_LIMIT)


def _resident(shape, layer=None):
    if layer is None:
        return pl.BlockSpec(shape, lambda *_: (0,) * len(shape), pipeline_mode=pl.Buffered(1))
    return pl.BlockSpec((None,) + tuple(shape), lambda *_: (layer,) + (0,) * len(shape),
                        pipeline_mode=pl.Buffered(1))


def _mod_spec(layer):
    return pl.BlockSpec((None, None, N_MOD, D_MODEL), lambda b, i: (layer, b, 0, 0))


def _sub_tiles(rows, sub=SUB_TILE):
    return [slice(s * sub, (s + 1) * sub) for s in range(rows // sub)]


def _mod_kernel(c_ref, w_ref, b_ref, o_ref):
    c = c_ref[...]
    cond = c * _sigmoid(c)
    o_ref[...] = _dot(cond.astype(_MXU), w_ref[...].astype(_MXU)) + b_ref[...]


def _modulation(c, ada_w, ada_b):
    depth = ada_w.shape[0]
    batch = c.shape[0]
    out = pl.pallas_call(
        _mod_kernel,
        out_shape=jax.ShapeDtypeStruct((depth, N_MOD, batch, D_MODEL), _F32),
        grid=(depth, N_MOD),
        in_specs=[
            pl.BlockSpec((batch, D_MODEL), lambda l, j: (0, 0)),
            pl.BlockSpec((None, D_MODEL, D_MODEL), lambda l, j: (l, 0, j)),
            pl.BlockSpec((None, None, 1, D_MODEL), lambda l, j: (l, j, 0, 0)),
        ],
        out_specs=pl.BlockSpec((None, None, batch, D_MODEL), lambda l, j: (l, j, 0, 0)),
        compiler_params=_params(("arbitrary", "arbitrary")),
        name="adaln_mod",
    )(c, ada_w, ada_b.reshape(depth, N_MOD, 1, D_MODEL))
    return jnp.transpose(out, (0, 2, 1, 3))


def _modulated_norm(x, mod_ref, gain_ref, piece):
    sh = mod_ref[3 * piece:3 * piece + 1, :]
    sc = mod_ref[3 * piece + 1:3 * piece + 2, :]
    return _rms(x, gain_ref[...]) * (1.0 + sc) + sh


def _ffn_kernel(x_ref, mod_ref, gain_ref, win_ref, wout_ref, *rest, piece, final):
    o_ref = rest[-1]
    g = mod_ref[3 * piece + 2:3 * piece + 3, :]
    subs = _sub_tiles(FFN_ROWS)
    hbs = [_modulated_norm(x_ref[rows, :], mod_ref, gain_ref, piece).astype(_MXU) for rows in subs]
    for rows, hb in zip(subs, hbs):
        acc = None
        for lo, hi in zip(FF_SPLITS[:-1], FF_SPLITS[1:]):
            gate = _dot(hb, win_ref[:, lo:hi])
            up = _dot(hb, win_ref[:, D_FF + lo:D_FF + hi])
            act = (gate * _sigmoid(gate) * up).astype(_MXU)
            part = _dot(act, wout_ref[lo:hi, :])
            acc = part if acc is None else acc + part
        y = x_ref[rows, :] + (0.5 * g) * acc
        if final:
            y = _rms(y, rest[0][...])
        o_ref[rows, :] = y


def _ffn(x, layer, mod, gain, w_in, w_out, piece, final_gain=None):
    batch, seq, _ = x.shape
    final = final_gain is not None
    in_specs = [
        pl.BlockSpec((None, FFN_ROWS, D_MODEL), lambda b, i: (b, i, 0)),
        _mod_spec(layer),
        _resident((1, D_MODEL), layer),
        _resident((D_MODEL, 2 * D_FF), layer),
        _resident((D_FF, D_MODEL), layer),
    ]
    args = [x, mod, gain, w_in, w_out]
    if final:
        in_specs.append(_resident((1, D_MODEL)))
        args.append(final_gain.reshape(1, D_MODEL))
    return pl.pallas_call(
        functools.partial(_ffn_kernel, piece=piece, final=final),
        out_shape=jax.ShapeDtypeStruct(x.shape, _F32),
        grid=(batch, seq // FFN_ROWS),
        in_specs=in_specs,
        out_specs=pl.BlockSpec((None, FFN_ROWS, D_MODEL), lambda b, i: (b, i, 0)),
        compiler_params=_params(("arbitrary", "arbitrary")),
        name="ffn_final" if final else "ffn",
    )(*args)


def _next_tile(b, i, batch, tiles):
    flat = jnp.minimum(b * tiles + i + 1, batch * tiles - 1)
    return flat // tiles, flat % tiles


def _next_specs(layer, batch, tiles):
    def x_map(b, i):
        nb, ni = _next_tile(b, i, batch, tiles)
        return nb, ni, 0

    def mod_map(b, i):
        nb, _ = _next_tile(b, i, batch, tiles)
        return layer, nb, 0, 0

    return (pl.BlockSpec((None, MIX_ROWS, D_MODEL), x_map),
            pl.BlockSpec((None, None, N_MOD, D_MODEL), mod_map))


class _Tasks:
    def __init__(self, pieces, steps):
        self._pieces = iter(pieces)
        self._per_step = -(-len(pieces) // steps)

    def run(self, k=None):
        for _ in range(self._per_step if k is None else k):
            piece = next(self._pieces, None)
            if piece is not None:
                piece()

    def finish(self):
        for piece in self._pieces:
            piece()


def _pipelined(first_pieces, next_pieces, work, steps):
    b = pl.program_id(0)
    i = pl.program_id(1)

    @pl.when((b == 0) & (i == 0))
    def _():
        for piece in first_pieces(0):
            piece()

    for parity in (0, 1):
        @pl.when(i % 2 == parity)
        def _(parity=parity):
            tasks = _Tasks(next_pieces(1 - parity), steps)
            work(parity, tasks)
            tasks.finish()


def _col_chunks(lo, hi):
    return [(c, min(c + PROJ_COLS, hi)) for c in range(lo, hi, PROJ_COLS)]


def _swa_project_pieces(x_ref, mod_ref, gain_ref, w_ref, slot):
    q_s, k_s, ks_s, vt_s, gate_s = slot
    pieces = []
    for rows in _sub_tiles(MIX_ROWS, MIX_SUB):
        hb = []

        def norm(rows=rows, hb=hb):
            hb.append(_modulated_norm(x_ref[rows, :], mod_ref, gain_ref, 1).astype(_MXU))

        def kv_proj(rows=rows, hb=hb):
            kv = _dot(hb[0], w_ref[:, _A_KV:_A_G])
            k = kv[:, :ATT_KV_W] * (HEAD_DIM ** -0.5)
            k_s[rows, :] = k.astype(k_s.dtype)
            ks_s[rows, :] = pltpu.roll(k, HEAD_DIM, axis=1).astype(ks_s.dtype)
            vt_s[:, rows] = kv[:, ATT_KV_W:].T.astype(vt_s.dtype)

        def q_proj(lo, hi, rows=rows, hb=hb):
            q_s[rows, lo - _A_Q:hi - _A_Q] = _dot(hb[0], w_ref[:, lo:hi]).astype(q_s.dtype)

        def gate_proj(lo, hi, rows=rows, hb=hb):
            gate_s[rows, lo - _A_G:hi - _A_G] = _sigmoid(_dot(hb[0], w_ref[:, lo:hi])).astype(gate_s.dtype)

        pieces += [norm, kv_proj]
        pieces += [functools.partial(q_proj, lo, hi) for lo, hi in _col_chunks(_A_Q, _A_KV)]
        pieces += [functools.partial(gate_proj, lo, hi) for lo, hi in _col_chunks(_A_G, _A_END)]
    return pieces


SWA_STEPS = MIX_BLOCKS * N_KV_HEADS * 2
SWA_AHEAD = 2


def _swa_attend(sinks_ref, slot, carry, o_ref, sink_base, tasks):
    q_s, k_s, ks_s, vt_s, gate_s = slot
    ck, cks, cvt = carry
    n = pl.program_id(1)
    grp = N_Q_HEADS // N_KV_HEADS
    pairs = grp // 2
    band = 2 * SEQ_BLOCK

    k_all = jnp.concatenate([ck[...], k_s[...]], axis=0)
    ks_all = jnp.concatenate([cks[...], ks_s[...]], axis=0)
    vt_all = jnp.concatenate([cvt[...], vt_s[...]], axis=1)
    low = lax.broadcasted_iota(jnp.int32, (band, LANES), 1) < HEAD_DIM
    zero = jnp.zeros((band, LANES), k_all.dtype)
    ones = jnp.ones((HEAD_DIM, band), vt_all.dtype)

    kj = lax.broadcasted_iota(jnp.int32, (band, SEQ_BLOCK), 0)
    qi = lax.broadcasted_iota(jnp.int32, (band, SEQ_BLOCK), 1)
    in_window = (kj > qi) & (kj <= qi + WINDOW)

    def scores(j, kvh, parity):
        r0 = j * SEQ_BLOCK
        nat, swp = (k_all, ks_all) if kvh == 0 else (ks_all, k_all)
        if parity == 0:
            k_m = jnp.where(low, nat[r0:r0 + band], zero)
        else:
            k_m = jnp.where(low, zero, swp[r0:r0 + band])
        q = jnp.concatenate(
            [q_s[r0:r0 + SEQ_BLOCK, (kvh * pairs + p) * LANES:(kvh * pairs + p + 1) * LANES]
             for p in range(pairs)], axis=0)
        return _dot_nt(k_m, q)

    def attend(j, kvh, parity, s):
        r0 = j * SEQ_BLOCK
        valid = in_window if j > 0 else in_window & ((kj >= SEQ_BLOCK) | (n > 0))
        bias = jnp.where(valid, 0.0, -jnp.inf)
        v_h = vt_all[kvh * HEAD_DIM:(kvh + 1) * HEAD_DIM, r0:r0 + band]
        vt_m = jnp.concatenate([v_h, ones] if parity == 0 else [ones, v_h], axis=0)
        probs, tails = [], []
        for p in range(pairs):
            s_p = s[:, p * SEQ_BLOCK:(p + 1) * SEQ_BLOCK] + bias
            sink = jnp.full((1, SEQ_BLOCK), sinks_ref[sink_base + kvh * grp + 2 * p + parity], _F32)
            m = jnp.maximum(jnp.max(s_p, axis=0, keepdims=True), sink)
            probs.append(jnp.exp(s_p - m).astype(_MXU))
            tails.append(jnp.exp(sink - m))
        pv = _dot(vt_m, jnp.concatenate(probs, axis=1))
        if parity == 0:
            num, den = pv[:HEAD_DIM], pv[HEAD_DIM:HEAD_DIM + 1]
        else:
            num, den = pv[HEAD_DIM:], pv[:1]
        return num * (1.0 / (den + jnp.concatenate(tails, axis=1)))

    groups = [(j, kvh, parity) for j in range(MIX_BLOCKS) for kvh in range(N_KV_HEADS) for parity in range(2)]
    ahead = [scores(*g) for g in groups[:SWA_AHEAD]]
    halves = []
    for idx, (j, kvh, parity) in enumerate(groups):
        s = ahead.pop(0)
        if idx + SWA_AHEAD < len(groups):
            ahead.append(scores(*groups[idx + SWA_AHEAD]))
        halves.append(attend(j, kvh, parity, s))
        tasks.run()
        if parity == 1:
            out_t = jnp.concatenate(halves, axis=0)
            halves = []
            for p in range(pairs):
                rows = slice(j * SEQ_BLOCK, (j + 1) * SEQ_BLOCK)
                cols = slice((kvh * pairs + p) * LANES, (kvh * pairs + p + 1) * LANES)
                y = out_t[:, p * SEQ_BLOCK:(p + 1) * SEQ_BLOCK].T
                o_ref[rows, cols] = (gate_s[rows, cols].astype(_F32) * y).astype(o_ref.dtype)

    last = slice(MIX_ROWS - SEQ_BLOCK, MIX_ROWS)
    ck[...] = k_s[last, :]
    cks[...] = ks_s[last, :]
    cvt[...] = vt_s[:, last]


def _swa_kernel(sinks_ref, x0_ref, mod0_ref, xn_ref, modn_ref, gain_ref, w_ref, o_ref, *scratch, sink_base):
    slots = (scratch[0:5], scratch[5:10])
    carry = scratch[10:13]

    def zero_carry():
        for c in carry:
            c[...] = jnp.zeros_like(c)

    _pipelined(lambda s: [zero_carry] + _swa_project_pieces(x0_ref, mod0_ref, gain_ref, w_ref, slots[s]),
               lambda s: _swa_project_pieces(xn_ref, modn_ref, gain_ref, w_ref, slots[s]),
               lambda s, tasks: _swa_attend(sinks_ref, slots[s], carry, o_ref, sink_base, tasks),
               SWA_STEPS)


def _swa(x, layer, mod, gain, w_a, sinks):
    batch, seq, _ = x.shape
    tiles = seq // MIX_ROWS
    x_next, mod_next = _next_specs(layer, batch, tiles)
    slot = [pltpu.VMEM((MIX_ROWS, ATT_Q_W), _MXU), pltpu.VMEM((MIX_ROWS, ATT_KV_W), _MXU),
            pltpu.VMEM((MIX_ROWS, ATT_KV_W), _MXU), pltpu.VMEM((ATT_KV_W, MIX_ROWS), _MXU),
            pltpu.VMEM((MIX_ROWS, D_MODEL), _MXU)]
    carry = [pltpu.VMEM((SEQ_BLOCK, ATT_KV_W), _MXU), pltpu.VMEM((SEQ_BLOCK, ATT_KV_W), _MXU),
             pltpu.VMEM((ATT_KV_W, SEQ_BLOCK), _MXU)]
    return pl.pallas_call(
        functools.partial(_swa_kernel, sink_base=layer * N_Q_HEADS),
        out_shape=jax.ShapeDtypeStruct((batch, seq, D_MODEL), _MXU),
        grid=(batch, tiles),
        in_specs=[
            pl.BlockSpec(memory_space=pltpu.SMEM),
            pl.BlockSpec((None, MIX_ROWS, D_MODEL), lambda b, i: (0, 0, 0)),
            pl.BlockSpec((None, None, N_MOD, D_MODEL), lambda b, i: (layer, 0, 0, 0)),
            x_next, mod_next,
            _resident((1, D_MODEL), layer),
            _resident((D_MODEL, _A_END), layer),
        ],
        out_specs=pl.BlockSpec((None, MIX_ROWS, D_MODEL), lambda b, i: (b, i, 0)),
        scratch_shapes=slot + slot + carry,
        compiler_params=_params(("arbitrary", "arbitrary")),
        name="swa",
    )(sinks, x, mod, x, mod, gain, w_a)


def _gla_project_pieces(x_ref, mod_ref, gain_ref, w_ref, wg_ref, bg_ref, slot):
    q_s, k_s, g_s, v_s, sr_s, sgb_s = slot
    pieces = []
    for rows in _sub_tiles(MIX_ROWS, MIX_SUB):
        hb = []

        def norm(rows=rows, hb=hb):
            hb.append(_modulated_norm(x_ref[rows, :], mod_ref, gain_ref, 1).astype(_MXU))

        def decay_gate(rows=rows, hb=hb):
            lr = _dot(hb[0], w_ref[:, _B_LR:_B_END]).astype(_MXU)
            z = _dot(lr, wg_ref[...]) + bg_ref[...]
            log_sig = jnp.minimum(z, 0.0) - jnp.log1p(jnp.exp(-jnp.abs(z)))
            g_s[rows, :] = log_sig / GLA_GATE_NORMALIZER

        def proj(dst, base, act, lo, hi, rows=rows, hb=hb):
            dst[rows, lo - base:hi - base] = act(_dot(hb[0], w_ref[:, lo:hi])).astype(dst.dtype)

        pieces += [norm, decay_gate]
        for dst, lo, hi, act in ((sr_s, _B_R, _B_G, lambda r: r * _sigmoid(r)),
                                 (sgb_s, _B_G, _B_LR, _sigmoid),
                                 (q_s, _B_Q, _B_K, lambda t: t),
                                 (k_s, _B_K, _B_V, lambda t: t),
                                 (v_s, _B_V, _B_R, lambda t: t)):
            pieces += [functools.partial(proj, dst, lo, act, c0, c1) for c0, c1 in _col_chunks(lo, hi)]
    return pieces


def _rows_bcast(x, starts, reps):
    return jnp.concatenate(
        [jnp.broadcast_to(x[s:s + 1, :], (reps, x.shape[1])) for s in starts], axis=0)


GLA_STEPS = MIX_BLOCKS * GLA_HEADS
GLA_AHEAD = 1


def _gla_attend(slot, gain_ref, za_ref, x_ref, mod_ref, w_ref, o_ref, state_ref, merged_ref, tasks):
    q_ref, k_ref, g_ref, v_ref, sr_ref, sgb_ref = slot
    C = SEQ_BLOCK
    ri = lax.broadcasted_iota(jnp.int32, (C, C), 0)
    ci = lax.broadcasted_iota(jnp.int32, (C, C), 1)
    tri = jnp.where(ci <= ri, 1.0, 0.0).astype(_MXU)
    kzero = jnp.zeros((C, GLA_DK), _MXU)
    diag_mask = (ri // GLA_SUB == ci // GLA_SUB) & (ci <= ri)
    levels = []
    b = GLA_SUB
    while b < C:
        pair_mask = (ri // (2 * b) == ci // (2 * b)) & ((ri // b) % 2 == 1) & ((ci // b) % 2 == 0)
        right = (lax.broadcasted_iota(jnp.int32, (C, GLA_DK), 0) // b) % 2 == 1
        levels.append((b, pair_mask, right))
        b *= 2

    def cum_decay(c):
        g = g_ref[c * C:(c + 1) * C, :] * LOG2_E
        g1 = g.astype(_MXU)
        r1 = g - g1.astype(_F32)
        g2 = r1.astype(_MXU)
        g3 = (r1 - g2.astype(_F32)).astype(_MXU)
        return _dot(tri, g1) + _dot(tri, g2) + _dot(tri, g3)

    def intra(c, h, G_all):
        rows = slice(c * C, (c + 1) * C)
        ks = slice(h * GLA_DK, (h + 1) * GLA_DK)
        G = G_all[:, ks]
        q = q_ref[rows, ks] * (GLA_DK ** -0.5)
        k = k_ref[rows, ks]
        G_last = G[C - 1:C, :]
        G_first = _rows_bcast(G, range(0, C, GLA_SUB), GLA_SUB)
        def t_mxu(x):
            return jnp.transpose(x).astype(_MXU)

        terms = [((q * jnp.exp2(G - G_first)).astype(_MXU), t_mxu(k * jnp.exp2(G_first - G)), diag_mask)]
        for b, pair_mask, right in levels:
            ref = _rows_bcast(G, range(b - 1, C, 2 * b), 2 * b)
            e = jnp.exp2(jnp.where(right, G - ref, ref - G))
            terms.append(((q * e).astype(_MXU), t_mxu(k * e), pair_mask))
        a = None
        for (q1, k1, m1), (q2, k2, m2) in zip(terms[0::2], terms[1::2]):
            lhs = jnp.concatenate([q1, q2], axis=1)
            rhs = jnp.concatenate([jnp.concatenate([k1, kzero], axis=1),
                                   jnp.concatenate([kzero, k2], axis=1)], axis=0)
            prod = _dot(lhs, rhs)
            part = jnp.where(m1, prod[:, :C], 0.0) + jnp.where(m2, prod[:, C:], 0.0)
            a = part if a is None else a + part
        qg = (q * jnp.exp2(G)).astype(_MXU)
        kg = (k * jnp.exp2(G_last - G)).astype(_MXU)
        return a.astype(_MXU), qg, kg, jnp.exp2(G_last)

    def inter(c, h, a, qg, kg, dec):
        rows = slice(c * C, (c + 1) * C)
        vs = slice(h * GLA_DV, (h + 1) * GLA_DV)
        v = v_ref[rows, vs]
        state = state_ref[h]
        lhs = jnp.concatenate([a, qg], axis=1)
        rhs = jnp.concatenate([v, state.astype(_MXU)], axis=0)
        o = _dot(lhs, rhs)
        upd = _dot_tn(kg, v)
        dec_t = jnp.transpose(jnp.broadcast_to(dec, (C, GLA_DK)))
        state_ref[h] = state * jnp.concatenate([dec_t] * (GLA_DV // GLA_DK), axis=1) + upd
        yb = _rms(o, gain_ref[...]) * sr_ref[rows, vs].astype(_F32)
        merged = za_ref[rows, vs].astype(_F32) + sgb_ref[rows, vs].astype(_F32) * yb
        merged_ref[rows, vs] = merged.astype(_MXU)

    units = [(c, h) for c in range(MIX_BLOCKS) for h in range(GLA_HEADS)]
    decays = {}

    def prepare(c, h):
        if c not in decays:
            decays[c] = cum_decay(c)
        return intra(c, h, decays[c])

    ahead = [prepare(*u) for u in units[:GLA_AHEAD]]
    for idx, (c, h) in enumerate(units):
        cur = ahead.pop(0)
        if idx + GLA_AHEAD < len(units):
            ahead.append(prepare(*units[idx + GLA_AHEAD]))
        inter(c, h, *cur)
        tasks.run()
        if h == GLA_HEADS - 1 and (c + 1) % GLA_PROJ_CHUNKS == 0:
            rows = slice((c + 1 - GLA_PROJ_CHUNKS) * C, (c + 1) * C)
            out = _dot(merged_ref[rows, :], w_ref[...])
            o_ref[rows, :] = x_ref[rows, :] + mod_ref[5:6, :] * out


def _gla_kernel(x_ref, mod_ref, xn_ref, modn_ref, gain_ref, w_ref, wg_ref, bg_ref, gla_gain_ref, za_ref,
                wout_ref, o_ref, *scratch):
    slots = (scratch[0:6], scratch[6:12])
    state_ref, merged_ref = scratch[12:14]

    @pl.when(pl.program_id(1) == 0)
    def _():
        state_ref[...] = jnp.zeros_like(state_ref)

    def pieces(src_ref, src_mod_ref, s):
        return _gla_project_pieces(src_ref, src_mod_ref, gain_ref, w_ref, wg_ref, bg_ref, slots[s])

    _pipelined(lambda s: pieces(x_ref, mod_ref, s),
               lambda s: pieces(xn_ref, modn_ref, s),
               lambda s, tasks: _gla_attend(slots[s], gla_gain_ref, za_ref, x_ref, mod_ref, wout_ref, o_ref,
                                            state_ref, merged_ref, tasks),
               GLA_STEPS)


def _gla_merge(x, layer, mod, gain, w_b, w_gate_p, b_gate, gla_gain, za, w_out):
    batch, seq, _ = x.shape
    tiles = seq // MIX_ROWS
    x_next, mod_next = _next_specs(layer, batch, tiles)

    def tok(width):
        return pl.BlockSpec((None, MIX_ROWS, width), lambda b, i: (b, i, 0))

    slot = [pltpu.VMEM((MIX_ROWS, GLA_K_W), _F32), pltpu.VMEM((MIX_ROWS, GLA_K_W), _F32),
            pltpu.VMEM((MIX_ROWS, GLA_K_W), _F32), pltpu.VMEM((MIX_ROWS, GLA_V_W), _MXU),
            pltpu.VMEM((MIX_ROWS, GLA_V_W), _MXU), pltpu.VMEM((MIX_ROWS, D_MODEL), _MXU)]
    return pl.pallas_call(
        _gla_kernel,
        out_shape=jax.ShapeDtypeStruct(x.shape, _F32),
        grid=(batch, tiles),
        in_specs=[
            tok(D_MODEL), _mod_spec(layer), x_next, mod_next,
            _resident((1, D_MODEL), layer),
            _resident((D_MODEL, _B_END), layer),
            _resident((LANES, GLA_K_W), layer),
            _resident((1, GLA_K_W), layer),
            _resident((1, GLA_DV), layer),
            tok(D_MODEL),
            _resident((D_MODEL, D_MODEL), layer),
        ],
        out_specs=tok(D_MODEL),
        scratch_shapes=slot + slot + [pltpu.VMEM((GLA_HEADS, GLA_DK, GLA_DV), _F32),
                                      pltpu.VMEM((MIX_ROWS, D_MODEL), _MXU)],
        compiler_params=_params(("arbitrary", "arbitrary")),
        name="gla_merge",
    )(x, mod, x, mod, gain, w_b, w_gate_p, b_gate, gla_gain, za, w_out)


def _arrange_mix_w_in(w):
    sizes = (ATT_Q_W, ATT_KV_W, ATT_KV_W, GLA_K_W, GLA_K_W, GLA_V_W, GLA_RANK, GLA_V_W, D_MODEL, D_MODEL)
    offs = [0]
    for s in sizes:
        offs.append(offs[-1] + s)
    qa, ka, va, qb, kb, vb, lr, r, ga, gb = [w[..., offs[i]:offs[i + 1]] for i in range(len(sizes))]
    lr = jnp.pad(lr, ((0, 0), (0, 0), (0, LANES - GLA_RANK)))
    w_a = jnp.concatenate([qa, ka, va, ga], axis=-1).astype(_MXU)
    w_b = jnp.concatenate([qb, kb, vb, r, gb, lr], axis=-1).astype(_MXU)
    return w_a, w_b


def kernel(x, c, ada_w, ada_b, norm1, ffn1_w_in, ffn1_w_out, norm_mix, mix_w_in, attn_sinks,
           gla_w_gate, gla_b_gate, gla_out_norm, mix_w_out, norm2, ffn2_w_in, ffn2_w_out, final_norm):
    depth = ada_w.shape[0]
    seq = x.shape[1]
    assert seq % FFN_ROWS == 0 and (seq // MIX_ROWS) % 2 == 0 and x.shape[2] == D_MODEL
    ffn_w = [(w_in.astype(_MXU), w_out.astype(_MXU))
             for w_in, w_out in ((ffn1_w_in, ffn1_w_out), (ffn2_w_in, ffn2_w_out))]
    ffn_gain = [n.reshape(depth, 1, D_MODEL) for n in (norm1, norm2)]
    mix_gain = norm_mix.reshape(depth, 1, D_MODEL)
    w_a, w_b = _arrange_mix_w_in(mix_w_in)
    w_gate_p = jnp.pad(gla_w_gate, ((0, 0), (0, LANES - GLA_RANK), (0, 0))).astype(_MXU)
    b_gate = gla_b_gate.reshape(depth, 1, GLA_K_W)
    gla_gain = gla_out_norm.reshape(depth, 1, GLA_DV)
    w_out = mix_w_out.astype(_MXU)
    sinks = attn_sinks.reshape(depth * N_Q_HEADS)

    mod = _modulation(c, ada_w, ada_b)
    for l in range(depth):
        x = _ffn(x, l, mod, ffn_gain[0], *ffn_w[0], 0)
        za = _swa(x, l, mod, mix_gain, w_a, sinks)
        x = _gla_merge(x, l, mod, mix_gain, w_b, w_gate_p, b_gate, gla_gain, za, w_out)
        x = _ffn(x, l, mod, ffn_gain[1], *ffn_w[1], 2,
                 final_gain=final_norm if l == depth - 1 else None)
    return x
```

```python
import functools

import jax
import jax.numpy as jnp
from jax import lax
from jax.experimental import pallas as pl
from jax.experimental.pallas import tpu as pltpu

D_MODEL = 1024
N_Q_HEADS = 16
N_KV_HEADS = 2
HEAD_DIM = 64
WINDOW = 128
GLA_HEADS = 4
GLA_DK = 128
GLA_DV = 256
GLA_RANK = 16
GLA_GATE_NORMALIZER = 16.0
GLA_SUB = 16
D_FF = 2816
N_MOD = 9
EPS = 1e-6
LOG2_E = 1.4426950408889634

ATT_Q_W = N_Q_HEADS * HEAD_DIM
ATT_KV_W = N_KV_HEADS * HEAD_DIM
GLA_K_W = GLA_HEADS * GLA_DK
GLA_V_W = GLA_HEADS * GLA_DV

LANES = 128
MXU_WIDTH = 256
SEQ_BLOCK = 128
MIX_ROWS = 512
MIX_BLOCKS = MIX_ROWS // SEQ_BLOCK
GLA_PROJ_CHUNKS = 2
SUB_TILE = 256
MIX_SUB = 256
PROJ_COLS = MXU_WIDTH
FFN_ROWS = 1024
FF_SPLITS = (0, 6 * MXU_WIDTH, D_FF)
VMEM_LIMIT = 56 * 1024 * 1024

_MXU = jnp.bfloat16
_F32 = jnp.float32

_A_Q = 0
_A_KV = _A_Q + ATT_Q_W
_A_G = _A_KV + 2 * ATT_KV_W
_A_END = _A_G + D_MODEL
_B_Q = 0
_B_K = _B_Q + GLA_K_W
_B_V = _B_K + GLA_K_W
_B_R = _B_V + GLA_V_W
_B_G = _B_R + GLA_V_W
_B_LR = _B_G + D_MODEL
_B_END = _B_LR + LANES


def _dot(a, b):
    return jnp.dot(a, b, preferred_element_type=_F32)


def _dot_nt(a, b):
    return lax.dot_general(a, b, (((1,), (1,)), ((), ())), preferred_element_type=_F32)


def _dot_tn(a, b):
    return lax.dot_general(a, b, (((0,), (0,)), ((), ())), preferred_element_type=_F32)


def _sigmoid(x):
    return 1.0 / (1.0 + jnp.exp(-x))


def _rms(x, gain):
    ms = jnp.mean(x * x, axis=-1, keepdims=True)
    return x * lax.rsqrt(ms + EPS) * gain


def _params(sem):
    return pltpu.CompilerParams(dimension_semantics=sem, vmem_limit_bytes=VMEM_LIMIT)


def _resident(shape, layer=None):
    if layer is None:
        return pl.BlockSpec(shape, lambda *_: (0,) * len(shape), pipeline_mode=pl.Buffered(1))
    return pl.BlockSpec((None,) + tuple(shape), lambda *_: (layer,) + (0,) * len(shape),
                        pipeline_mode=pl.Buffered(1))


def _mod_spec(layer):
    return pl.BlockSpec((None, None, N_MOD, D_MODEL), lambda b, i: (layer, b, 0, 0))


def _sub_tiles(rows, sub=SUB_TILE):
    return [slice(s * sub, (s + 1) * sub) for s in range(rows // sub)]


def _mod_kernel(c_ref, w_ref, b_ref, o_ref):
    c = c_ref[...]
    cond = c * _sigmoid(c)
    o_ref[...] = _dot(cond.astype(_MXU), w_ref[...].astype(_MXU)) + b_ref[...]


def _modulation(c, ada_w, ada_b):
    depth = ada_w.shape[0]
    batch = c.shape[0]
    out = pl.pallas_call(
        _mod_kernel,
        out_shape=jax.ShapeDtypeStruct((depth, N_MOD, batch, D_MODEL), _F32),
        grid=(depth, N_MOD),
        in_specs=[
            pl.BlockSpec((batch, D_MODEL), lambda l, j: (0, 0)),
            pl.BlockSpec((None, D_MODEL, D_MODEL), lambda l, j: (l, 0, j)),
            pl.BlockSpec((None, None, 1, D_MODEL), lambda l, j: (l, j, 0, 0)),
        ],
        out_specs=pl.BlockSpec((None, None, batch, D_MODEL), lambda l, j: (l, j, 0, 0)),
        compiler_params=_params(("arbitrary", "arbitrary")),
        name="adaln_mod",
    )(c, ada_w, ada_b.reshape(depth, N_MOD, 1, D_MODEL))
    return jnp.transpose(out, (0, 2, 1, 3))


def _modulated_norm(x, mod_ref, gain_ref, piece):
    sh = mod_ref[3 * piece:3 * piece + 1, :]
    sc = mod_ref[3 * piece + 1:3 * piece + 2, :]
    return _rms(x, gain_ref[...]) * (1.0 + sc) + sh


def _ffn_kernel(x_ref, mod_ref, gain_ref, win_ref, wout_ref, *rest, piece, final):
    o_ref = rest[-1]
    g = mod_ref[3 * piece + 2:3 * piece + 3, :]
    subs = _sub_tiles(FFN_ROWS)
    hbs = [_modulated_norm(x_ref[rows, :], mod_ref, gain_ref, piece).astype(_MXU) for rows in subs]
    for rows, hb in zip(subs, hbs):
        acc = None
        for lo, hi in zip(FF_SPLITS[:-1], FF_SPLITS[1:]):
            gate = _dot(hb, win_ref[:, lo:hi])
            up = _dot(hb, win_ref[:, D_FF + lo:D_FF + hi])
            act = (gate * _sigmoid(gate) * up).astype(_MXU)
            part = _dot(act, wout_ref[lo:hi, :])
            acc = part if acc is None else acc + part
        y = x_ref[rows, :] + (0.5 * g) * acc
        if final:
            y = _rms(y, rest[0][...])
        o_ref[rows, :] = y


def _ffn(x, layer, mod, gain, w_in, w_out, piece, final_gain=None):
    batch, seq, _ = x.shape
    final = final_gain is not None
    in_specs = [
        pl.BlockSpec((None, FFN_ROWS, D_MODEL), lambda b, i: (b, i, 0)),
        _mod_spec(layer),
        _resident((1, D_MODEL), layer),
        _resident((D_MODEL, 2 * D_FF), layer),
        _resident((D_FF, D_MODEL), layer),
    ]
    args = [x, mod, gain, w_in, w_out]
    if final:
        in_specs.append(_resident((1, D_MODEL)))
        args.append(final_gain.reshape(1, D_MODEL))
    return pl.pallas_call(
        functools.partial(_ffn_kernel, piece=piece, final=final),
        out_shape=jax.ShapeDtypeStruct(x.shape, _F32),
        grid=(batch, seq // FFN_ROWS),
        in_specs=in_specs,
        out_specs=pl.BlockSpec((None, FFN_ROWS, D_MODEL), lambda b, i: (b, i, 0)),
        compiler_params=_params(("arbitrary", "arbitrary")),
        name="ffn_final" if final else "ffn",
    )(*args)


def _next_tile(b, i, batch, tiles):
    flat = jnp.minimum(b * tiles + i + 1, batch * tiles - 1)
    return flat // tiles, flat % tiles


def _next_specs(layer, batch, tiles):
    def x_map(b, i):
        nb, ni = _next_tile(b, i, batch, tiles)
        return nb, ni, 0

    def mod_map(b, i):
        nb, _ = _next_tile(b, i, batch, tiles)
        return layer, nb, 0, 0

    return (pl.BlockSpec((None, MIX_ROWS, D_MODEL), x_map),
            pl.BlockSpec((None, None, N_MOD, D_MODEL), mod_map))


class _Tasks:
    def __init__(self, pieces, steps):
        self._pieces = iter(pieces)
        self._per_step = -(-len(pieces) // steps)

    def run(self, k=None):
        for _ in range(self._per_step if k is None else k):
            piece = next(self._pieces, None)
            if piece is not None:
                piece()

    def finish(self):
        for piece in self._pieces:
            piece()


def _pipelined(first_pieces, next_pieces, work, steps):
    b = pl.program_id(0)
    i = pl.program_id(1)

    @pl.when((b == 0) & (i == 0))
    def _():
        for piece in first_pieces(0):
            piece()

    for parity in (0, 1):
        @pl.when(i % 2 == parity)
        def _(parity=parity):
            tasks = _Tasks(next_pieces(1 - parity), steps)
            work(parity, tasks)
            tasks.finish()


def _col_chunks(lo, hi):
    return [(c, min(c + PROJ_COLS, hi)) for c in range(lo, hi, PROJ_COLS)]


def _swa_project_pieces(x_ref, mod_ref, gain_ref, w_ref, slot):
    q_s, k_s, ks_s, vt_s, gate_s = slot
    pieces = []
    for rows in _sub_tiles(MIX_ROWS, MIX_SUB):
        hb = []

        def norm(rows=rows, hb=hb):
            hb.append(_modulated_norm(x_ref[rows, :], mod_ref, gain_ref, 1).astype(_MXU))

        def kv_proj(rows=rows, hb=hb):
            kv = _dot(hb[0], w_ref[:, _A_KV:_A_G])
            k = kv[:, :ATT_KV_W] * (HEAD_DIM ** -0.5)
            k_s[rows, :] = k.astype(k_s.dtype)
            ks_s[rows, :] = pltpu.roll(k, HEAD_DIM, axis=1).astype(ks_s.dtype)
            vt_s[:, rows] = kv[:, ATT_KV_W:].T.astype(vt_s.dtype)

        def q_proj(lo, hi, rows=rows, hb=hb):
            q_s[rows, lo - _A_Q:hi - _A_Q] = _dot(hb[0], w_ref[:, lo:hi]).astype(q_s.dtype)

        def gate_proj(lo, hi, rows=rows, hb=hb):
            gate_s[rows, lo - _A_G:hi - _A_G] = _sigmoid(_dot(hb[0], w_ref[:, lo:hi])).astype(gate_s.dtype)

        pieces += [norm, kv_proj]
        pieces += [functools.partial(q_proj, lo, hi) for lo, hi in _col_chunks(_A_Q, _A_KV)]
        pieces += [functools.partial(gate_proj, lo, hi) for lo, hi in _col_chunks(_A_G, _A_END)]
    return pieces


SWA_STEPS = MIX_BLOCKS * N_KV_HEADS * 2
SWA_AHEAD = 2


def _swa_attend(sinks_ref, slot, carry, o_ref, sink_base, tasks):
    q_s, k_s, ks_s, vt_s, gate_s = slot
    ck, cks, cvt = carry
    n = pl.program_id(1)
    grp = N_Q_HEADS // N_KV_HEADS
    pairs = grp // 2
    band = 2 * SEQ_BLOCK

    k_all = jnp.concatenate([ck[...], k_s[...]], axis=0)
    ks_all = jnp.concatenate([cks[...], ks_s[...]], axis=0)
    vt_all = jnp.concatenate([cvt[...], vt_s[...]], axis=1)
    low = lax.broadcasted_iota(jnp.int32, (band, LANES), 1) < HEAD_DIM
    zero = jnp.zeros((band, LANES), k_all.dtype)
    ones = jnp.ones((HEAD_DIM, band), vt_all.dtype)

    kj = lax.broadcasted_iota(jnp.int32, (band, SEQ_BLOCK), 0)
    qi = lax.broadcasted_iota(jnp.int32, (band, SEQ_BLOCK), 1)
    in_window = (kj > qi) & (kj <= qi + WINDOW)

    def scores(j, kvh, parity):
        r0 = j * SEQ_BLOCK
        nat, swp = (k_all, ks_all) if kvh == 0 else (ks_all, k_all)
        if parity == 0:
            k_m = jnp.where(low, nat[r0:r0 + band], zero)
        else:
            k_m = jnp.where(low, zero, swp[r0:r0 + band])
        q = jnp.concatenate(
            [q_s[r0:r0 + SEQ_BLOCK, (kvh * pairs + p) * LANES:(kvh * pairs + p + 1) * LANES]
             for p in range(pairs)], axis=0)
        return _dot_nt(k_m, q)

    def attend(j, kvh, parity, s):
        r0 = j * SEQ_BLOCK
        valid = in_window if j > 0 else in_window & ((kj >= SEQ_BLOCK) | (n > 0))
        bias = jnp.where(valid, 0.0, -jnp.inf)
        v_h = vt_all[kvh * HEAD_DIM:(kvh + 1) * HEAD_DIM, r0:r0 + band]
        vt_m = jnp.concatenate([v_h, ones] if parity == 0 else [ones, v_h], axis=0)
        probs, tails = [], []
        for p in range(pairs):
            s_p = s[:, p * SEQ_BLOCK:(p + 1) * SEQ_BLOCK] + bias
            sink = jnp.full((1, SEQ_BLOCK), sinks_ref[sink_base + kvh * grp + 2 * p + parity], _F32)
            m = jnp.maximum(jnp.max(s_p, axis=0, keepdims=True), sink)
            probs.append(jnp.exp(s_p - m).astype(_MXU))
            tails.append(jnp.exp(sink - m))
        pv = _dot(vt_m, jnp.concatenate(probs, axis=1))
        if parity == 0:
            num, den = pv[:HEAD_DIM], pv[HEAD_DIM:HEAD_DIM + 1]
        else:
            num, den = pv[HEAD_DIM:], pv[:1]
        return num * (1.0 / (den + jnp.concatenate(tails, axis=1)))

    groups = [(j, kvh, parity) for j in range(MIX_BLOCKS) for kvh in range(N_KV_HEADS) for parity in range(2)]
    ahead = [scores(*g) for g in groups[:SWA_AHEAD]]
    halves = []
    for idx, (j, kvh, parity) in enumerate(groups):
        s = ahead.pop(0)
        if idx + SWA_AHEAD < len(groups):
            ahead.append(scores(*groups[idx + SWA_AHEAD]))
        halves.append(attend(j, kvh, parity, s))
        tasks.run()
        if parity == 1:
            out_t = jnp.concatenate(halves, axis=0)
            halves = []
            for p in range(pairs):
                rows = slice(j * SEQ_BLOCK, (j + 1) * SEQ_BLOCK)
                cols = slice((kvh * pairs + p) * LANES, (kvh * pairs + p + 1) * LANES)
                y = out_t[:, p * SEQ_BLOCK:(p + 1) * SEQ_BLOCK].T
                o_ref[rows, cols] = (gate_s[rows, cols].astype(_F32) * y).astype(o_ref.dtype)

    last = slice(MIX_ROWS - SEQ_BLOCK, MIX_ROWS)
    ck[...] = k_s[last, :]
    cks[...] = ks_s[last, :]
    cvt[...] = vt_s[:, last]


def _swa_kernel(sinks_ref, x0_ref, mod0_ref, xn_ref, modn_ref, gain_ref, w_ref, o_ref, *scratch, sink_base):
    slots = (scratch[0:5], scratch[5:10])
    carry = scratch[10:13]

    def zero_carry():
        for c in carry:
            c[...] = jnp.zeros_like(c)

    _pipelined(lambda s: [zero_carry] + _swa_project_pieces(x0_ref, mod0_ref, gain_ref, w_ref, slots[s]),
               lambda s: _swa_project_pieces(xn_ref, modn_ref, gain_ref, w_ref, slots[s]),
               lambda s, tasks: _swa_attend(sinks_ref, slots[s], carry, o_ref, sink_base, tasks),
               SWA_STEPS)


def _swa(x, layer, mod, gain, w_a, sinks):
    batch, seq, _ = x.shape
    tiles = seq // MIX_ROWS
    x_next, mod_next = _next_specs(layer, batch, tiles)
    slot = [pltpu.VMEM((MIX_ROWS, ATT_Q_W), _MXU), pltpu.VMEM((MIX_ROWS, ATT_KV_W), _MXU),
            pltpu.VMEM((MIX_ROWS, ATT_KV_W), _MXU), pltpu.VMEM((ATT_KV_W, MIX_ROWS), _MXU),
            pltpu.VMEM((MIX_ROWS, D_MODEL), _MXU)]
    carry = [pltpu.VMEM((SEQ_BLOCK, ATT_KV_W), _MXU), pltpu.VMEM((SEQ_BLOCK, ATT_KV_W), _MXU),
             pltpu.VMEM((ATT_KV_W, SEQ_BLOCK), _MXU)]
    return pl.pallas_call(
        functools.partial(_swa_kernel, sink_base=layer * N_Q_HEADS),
        out_shape=jax.ShapeDtypeStruct((batch, seq, D_MODEL), _MXU),
        grid=(batch, tiles),
        in_specs=[
            pl.BlockSpec(memory_space=pltpu.SMEM),
            pl.BlockSpec((None, MIX_ROWS, D_MODEL), lambda b, i: (0, 0, 0)),
            pl.BlockSpec((None, None, N_MOD, D_MODEL), lambda b, i: (layer, 0, 0, 0)),
            x_next, mod_next,
            _resident((1, D_MODEL), layer),
            _resident((D_MODEL, _A_END), layer),
        ],
        out_specs=pl.BlockSpec((None, MIX_ROWS, D_MODEL), lambda b, i: (b, i, 0)),
        scratch_shapes=slot + slot + carry,
        compiler_params=_params(("arbitrary", "arbitrary")),
        name="swa",
    )(sinks, x, mod, x, mod, gain, w_a)


def _gla_project_pieces(x_ref, mod_ref, gain_ref, w_ref, wg_ref, bg_ref, slot):
    q_s, k_s, g_s, v_s, sr_s, sgb_s = slot
    pieces = []
    for rows in _sub_tiles(MIX_ROWS, MIX_SUB):
        hb = []

        def norm(rows=rows, hb=hb):
            hb.append(_modulated_norm(x_ref[rows, :], mod_ref, gain_ref, 1).astype(_MXU))

        def decay_gate(rows=rows, hb=hb):
            lr = _dot(hb[0], w_ref[:, _B_LR:_B_END]).astype(_MXU)
            z = _dot(lr, wg_ref[...]) + bg_ref[...]
            log_sig = jnp.minimum(z, 0.0) - jnp.log1p(jnp.exp(-jnp.abs(z)))
            g_s[rows, :] = log_sig / GLA_GATE_NORMALIZER

        def proj(dst, base, act, lo, hi, rows=rows, hb=hb):
            dst[rows, lo - base:hi - base] = act(_dot(hb[0], w_ref[:, lo:hi])).astype(dst.dtype)

        pieces += [norm, decay_gate]
        for dst, lo, hi, act in ((sr_s, _B_R, _B_G, lambda r: r * _sigmoid(r)),
                                 (sgb_s, _B_G, _B_LR, _sigmoid),
                                 (q_s, _B_Q, _B_K, lambda t: t),
                                 (k_s, _B_K, _B_V, lambda t: t),
                                 (v_s, _B_V, _B_R, lambda t: t)):
            pieces += [functools.partial(proj, dst, lo, act, c0, c1) for c0, c1 in _col_chunks(lo, hi)]
    return pieces


def _rows_bcast(x, starts, reps):
    return jnp.concatenate(
        [jnp.broadcast_to(x[s:s + 1, :], (reps, x.shape[1])) for s in starts], axis=0)


GLA_STEPS = MIX_BLOCKS * GLA_HEADS
GLA_AHEAD = 1


def _gla_attend(slot, gain_ref, za_ref, x_ref, mod_ref, w_ref, o_ref, state_ref, merged_ref, tasks):
    q_ref, k_ref, g_ref, v_ref, sr_ref, sgb_ref = slot
    C = SEQ_BLOCK
    ri = lax.broadcasted_iota(jnp.int32, (C, C), 0)
    ci = lax.broadcasted_iota(jnp.int32, (C, C), 1)
    tri = jnp.where(ci <= ri, 1.0, 0.0).astype(_MXU)
    kzero = jnp.zeros((C, GLA_DK), _MXU)
    diag_mask = (ri // GLA_SUB == ci // GLA_SUB) & (ci <= ri)
    levels = []
    b = GLA_SUB
    while b < C:
        pair_mask = (ri // (2 * b) == ci // (2 * b)) & ((ri // b) % 2 == 1) & ((ci // b) % 2 == 0)
        right = (lax.broadcasted_iota(jnp.int32, (C, GLA_DK), 0) // b) % 2 == 1
        levels.append((b, pair_mask, right))
        b *= 2

    def cum_decay(c):
        g = g_ref[c * C:(c + 1) * C, :] * LOG2_E
        g1 = g.astype(_MXU)
        r1 = g - g1.astype(_F32)
        g2 = r1.astype(_MXU)
        g3 = (r1 - g2.astype(_F32)).astype(_MXU)
        return _dot(tri, g1) + _dot(tri, g2) + _dot(tri, g3)

    def intra(c, h, G_all):
        rows = slice(c * C, (c + 1) * C)
        ks = slice(h * GLA_DK, (h + 1) * GLA_DK)
        G = G_all[:, ks]
        q = q_ref[rows, ks] * (GLA_DK ** -0.5)
        k = k_ref[rows, ks]
        G_last = G[C - 1:C, :]
        G_first = _rows_bcast(G, range(0, C, GLA_SUB), GLA_SUB)
        def t_mxu(x):
            return jnp.transpose(x).astype(_MXU)

        terms = [((q * jnp.exp2(G - G_first)).astype(_MXU), t_mxu(k * jnp.exp2(G_first - G)), diag_mask)]
        for b, pair_mask, right in levels:
            ref = _rows_bcast(G, range(b - 1, C, 2 * b), 2 * b)
            e = jnp.exp2(jnp.where(right, G - ref, ref - G))
            terms.append(((q * e).astype(_MXU), t_mxu(k * e), pair_mask))
        a = None
        for (q1, k1, m1), (q2, k2, m2) in zip(terms[0::2], terms[1::2]):
            lhs = jnp.concatenate([q1, q2], axis=1)
            rhs = jnp.concatenate([jnp.concatenate([k1, kzero], axis=1),
                                   jnp.concatenate([kzero, k2], axis=1)], axis=0)
            prod = _dot(lhs, rhs)
            part = jnp.where(m1, prod[:, :C], 0.0) + jnp.where(m2, prod[:, C:], 0.0)
            a = part if a is None else a + part
        qg = (q * jnp.exp2(G)).astype(_MXU)
        kg = (k * jnp.exp2(G_last - G)).astype(_MXU)
        return a.astype(_MXU), qg, kg, jnp.exp2(G_last)

    def inter(c, h, a, qg, kg, dec):
        rows = slice(c * C, (c + 1) * C)
        vs = slice(h * GLA_DV, (h + 1) * GLA_DV)
        v = v_ref[rows, vs]
        state = state_ref[h]
        lhs = jnp.concatenate([a, qg], axis=1)
        rhs = jnp.concatenate([v, state.astype(_MXU)], axis=0)
        o = _dot(lhs, rhs)
        upd = _dot_tn(kg, v)
        dec_t = jnp.transpose(jnp.broadcast_to(dec, (C, GLA_DK)))
        state_ref[h] = state * jnp.concatenate([dec_t] * (GLA_DV // GLA_DK), axis=1) + upd
        yb = _rms(o, gain_ref[...]) * sr_ref[rows, vs].astype(_F32)
        merged = za_ref[rows, vs].astype(_F32) + sgb_ref[rows, vs].astype(_F32) * yb
        merged_ref[rows, vs] = merged.astype(_MXU)

    units = [(c, h) for c in range(MIX_BLOCKS) for h in range(GLA_HEADS)]
    decays = {}

    def prepare(c, h):
        if c not in decays:
            decays[c] = cum_decay(c)
        return intra(c, h, decays[c])

    ahead = [prepare(*u) for u in units[:GLA_AHEAD]]
    for idx, (c, h) in enumerate(units):
        cur = ahead.pop(0)
        if idx + GLA_AHEAD < len(units):
            ahead.append(prepare(*units[idx + GLA_AHEAD]))
        inter(c, h, *cur)
        tasks.run()
        if h == GLA_HEADS - 1 and (c + 1) % GLA_PROJ_CHUNKS == 0:
            rows = slice((c + 1 - GLA_PROJ_CHUNKS) * C, (c + 1) * C)
            out = _dot(merged_ref[rows, :], w_ref[...])
            o_ref[rows, :] = x_ref[rows, :] + mod_ref[5:6, :] * out


def _gla_kernel(x_ref, mod_ref, xn_ref, modn_ref, gain_ref, w_ref, wg_ref, bg_ref, gla_gain_ref, za_ref,
                wout_ref, o_ref, *scratch):
    slots = (scratch[0:6], scratch[6:12])
    state_ref, merged_ref = scratch[12:14]

    @pl.when(pl.program_id(1) == 0)
    def _():
        state_ref[...] = jnp.zeros_like(state_ref)

    def pieces(src_ref, src_mod_ref, s):
        return _gla_project_pieces(src_ref, src_mod_ref, gain_ref, w_ref, wg_ref, bg_ref, slots[s])

    _pipelined(lambda s: pieces(x_ref, mod_ref, s),
               lambda s: pieces(xn_ref, modn_ref, s),
               lambda s, tasks: _gla_attend(slots[s], gla_gain_ref, za_ref, x_ref, mod_ref, wout_ref, o_ref,
                                            state_ref, merged_ref, tasks),
               GLA_STEPS)


def _gla_merge(x, layer, mod, gain, w_b, w_gate_p, b_gate, gla_gain, za, w_out):
    batch, seq, _ = x.shape
    tiles = seq // MIX_ROWS
    x_next, mod_next = _next_specs(layer, batch, tiles)

    def tok(width):
        return pl.BlockSpec((None, MIX_ROWS, width), lambda b, i: (b, i, 0))

    slot = [pltpu.VMEM((MIX_ROWS, GLA_K_W), _F32), pltpu.VMEM((MIX_ROWS, GLA_K_W), _F32),
            pltpu.VMEM((MIX_ROWS, GLA_K_W), _F32), pltpu.VMEM((MIX_ROWS, GLA_V_W), _MXU),
            pltpu.VMEM((MIX_ROWS, GLA_V_W), _MXU), pltpu.VMEM((MIX_ROWS, D_MODEL), _MXU)]
    return pl.pallas_call(
        _gla_kernel,
        out_shape=jax.ShapeDtypeStruct(x.shape, _F32),
        grid=(batch, tiles),
        in_specs=[
            tok(D_MODEL), _mod_spec(layer), x_next, mod_next,
            _resident((1, D_MODEL), layer),
            _resident((D_MODEL, _B_END), layer),
            _resident((LANES, GLA_K_W), layer),
            _resident((1, GLA_K_W), layer),
            _resident((1, GLA_DV), layer),
            tok(D_MODEL),
            _resident((D_MODEL, D_MODEL), layer),
        ],
        out_specs=tok(D_MODEL),
        scratch_shapes=slot + slot + [pltpu.VMEM((GLA_HEADS, GLA_DK, GLA_DV), _F32),
                                      pltpu.VMEM((MIX_ROWS, D_MODEL), _MXU)],
        compiler_params=_params(("arbitrary", "arbitrary")),
        name="gla_merge",
    )(x, mod, x, mod, gain, w_b, w_gate_p, b_gate, gla_gain, za, w_out)


_IN_SIZES = (ATT_Q_W, ATT_KV_W, ATT_KV_W, GLA_K_W, GLA_K_W, GLA_V_W, GLA_RANK, GLA_V_W, D_MODEL, D_MODEL)
_IN_OFFS = tuple(sum(_IN_SIZES[:i]) for i in range(len(_IN_SIZES) + 1))
IN_COLS = _IN_OFFS[-1]
WPREP_ROWS = 256
WPREP_COLS = 512


def _wprep_kernel(wt_ref, wa_ref, wb_ref):
    o_qb, o_lr, o_r, o_ga, o_gb = _IN_OFFS[3], _IN_OFFS[6], _IN_OFFS[7], _IN_OFFS[8], _IN_OFFS[9]

    def cols(lo, width):
        return jnp.transpose(wt_ref[lo:lo + width, :])

    def put(dst, dst_lo, src_lo, width):
        for c in range(0, width, WPREP_COLS):
            n = min(WPREP_COLS, width - c)
            dst[:, dst_lo + c:dst_lo + c + n] = cols(src_lo + c, n).astype(_MXU)

    put(wa_ref, _A_Q, 0, o_qb)
    put(wa_ref, _A_G, o_ga, D_MODEL)
    put(wb_ref, _B_Q, o_qb, o_lr - o_qb)
    put(wb_ref, _B_R, o_r, GLA_V_W)
    put(wb_ref, _B_G, o_gb, D_MODEL)
    lr = cols(o_lr, LANES)
    keep = lax.broadcasted_iota(jnp.int32, lr.shape, 1) < GLA_RANK
    wb_ref[:, _B_LR:_B_END] = jnp.where(keep, lr, 0.0).astype(_MXU)


def _arrange_mix_w_in(w):
    depth = w.shape[0]
    assert w.shape[2] == IN_COLS and all(o % 8 == 0 for o in _IN_OFFS)
    return pl.pallas_call(
        _wprep_kernel,
        out_shape=[jax.ShapeDtypeStruct((depth, D_MODEL, _A_END), _MXU),
                   jax.ShapeDtypeStruct((depth, D_MODEL, _B_END), _MXU)],
        grid=(depth, D_MODEL // WPREP_ROWS),
        in_specs=[pl.BlockSpec((None, IN_COLS, WPREP_ROWS), lambda l, i: (l, 0, i))],
        out_specs=[pl.BlockSpec((None, WPREP_ROWS, _A_END), lambda l, i: (l, i, 0)),
                   pl.BlockSpec((None, WPREP_ROWS, _B_END), lambda l, i: (l, i, 0))],
        compiler_params=_params(("arbitrary", "arbitrary")),
        name="mix_weight_prep",
    )(jnp.swapaxes(w, 1, 2))


def kernel(x, c, ada_w, ada_b, norm1, ffn1_w_in, ffn1_w_out, norm_mix, mix_w_in, attn_sinks,
           gla_w_gate, gla_b_gate, gla_out_norm, mix_w_out, norm2, ffn2_w_in, ffn2_w_out, final_norm):
    depth = ada_w.shape[0]
    seq = x.shape[1]
    assert seq % FFN_ROWS == 0 and (seq // MIX_ROWS) % 2 == 0 and x.shape[2] == D_MODEL
    ffn_w = [(w_in.astype(_MXU), w_out.astype(_MXU))
             for w_in, w_out in ((ffn1_w_in, ffn1_w_out), (ffn2_w_in, ffn2_w_out))]
    ffn_gain = [n.reshape(depth, 1, D_MODEL) for n in (norm1, norm2)]
    mix_gain = norm_mix.reshape(depth, 1, D_MODEL)
    w_a, w_b = _arrange_mix_w_in(mix_w_in)
    w_gate_p = jnp.pad(gla_w_gate, ((0, 0), (0, LANES - GLA_RANK), (0, 0))).astype(_MXU)
    b_gate = gla_b_gate.reshape(depth, 1, GLA_K_W)
    gla_gain = gla_out_norm.reshape(depth, 1, GLA_DV)
    w_out = mix_w_out.astype(_MXU)
    sinks = attn_sinks.reshape(depth * N_Q_HEADS)

    mod = _modulation(c, ada_w, ada_b)
    for l in range(depth):
        x = _ffn(x, l, mod, ffn_gain[0], *ffn_w[0], 0)
        za = _swa(x, l, mod, mix_gain, w_a, sinks)
        x = _gla_merge(x, l, mod, mix_gain, w_b, w_gate_p, b_gate, gla_gain, za, w_out)
        x = _ffn(x, l, mod, ffn_gain[1], *ffn_w[1], 2,
                 final_gain=final_norm if l == depth - 1 else None)
    return x
```

```python
import functools

import jax
import jax.numpy as jnp
from jax import lax
from jax.experimental import pallas as pl
from jax.experimental.pallas import tpu as pltpu

D_MODEL = 1024
N_Q_HEADS = 16
N_KV_HEADS = 2
HEAD_DIM = 64
WINDOW = 128
GLA_HEADS = 4
GLA_DK = 128
GLA_DV = 256
GLA_RANK = 16
GLA_GATE_NORMALIZER = 16.0
GLA_SUB = 16
D_FF = 2816
N_MOD = 9
EPS = 1e-6
LOG2_E = 1.4426950408889634

ATT_Q_W = N_Q_HEADS * HEAD_DIM
ATT_KV_W = N_KV_HEADS * HEAD_DIM
GLA_K_W = GLA_HEADS * GLA_DK
GLA_V_W = GLA_HEADS * GLA_DV

LANES = 128
MXU_WIDTH = 256
SEQ_BLOCK = 128
MIX_ROWS = 256
MIX_BLOCKS = MIX_ROWS // SEQ_BLOCK
SWA_ROWS = 512
SWA_BLOCKS = SWA_ROWS // SEQ_BLOCK
GLA_PROJ_CHUNKS = 2
SUB_TILE = 256
MIX_SUB = 256
PROJ_COLS = MXU_WIDTH
FFN_ROWS = 1024
FF_SPLITS = (0, 6 * MXU_WIDTH, D_FF)
VMEM_LIMIT = 56 * 1024 * 1024

_MXU = jnp.bfloat16
_F32 = jnp.float32

_A_Q = 0
_A_KV = _A_Q + ATT_Q_W
_A_G = _A_KV + 2 * ATT_KV_W
_A_END = _A_G + D_MODEL
_B_Q = 0
_B_K = _B_Q + GLA_K_W
_B_V = _B_K + GLA_K_W
_B_R = _B_V + GLA_V_W
_B_G = _B_R + GLA_V_W
_B_LR = _B_G + D_MODEL
_B_END = _B_LR + LANES


def _dot(a, b):
    return jnp.dot(a, b, preferred_element_type=_F32)


def _dot_nt(a, b):
    return lax.dot_general(a, b, (((1,), (1,)), ((), ())), preferred_element_type=_F32)


def _dot_tn(a, b):
    return lax.dot_general(a, b, (((0,), (0,)), ((), ())), preferred_element_type=_F32)


def _sigmoid(x):
    return 1.0 / (1.0 + jnp.exp(-x))


def _rms(x, gain):
    ms = jnp.mean(x * x, axis=-1, keepdims=True)
    return x * lax.rsqrt(ms + EPS) * gain


def _params(sem):
    return pltpu.CompilerParams(dimension_semantics=sem, vmem_limit_bytes=VMEM_LIMIT)


def _resident(shape, layer=None):
    if layer is None:
        return pl.BlockSpec(shape, lambda *_: (0,) * len(shape), pipeline_mode=pl.Buffered(1))
    return pl.BlockSpec((None,) + tuple(shape), lambda *_: (layer,) + (0,) * len(shape),
                        pipeline_mode=pl.Buffered(1))


def _mod_spec(layer):
    return pl.BlockSpec((None, None, N_MOD, D_MODEL), lambda b, i: (layer, b, 0, 0))


def _sub_tiles(rows, sub=SUB_TILE):
    return [slice(s * sub, (s + 1) * sub) for s in range(rows // sub)]


def _mod_kernel(c_ref, w_ref, b_ref, o_ref):
    c = c_ref[...]
    cond = c * _sigmoid(c)
    o_ref[...] = _dot(cond.astype(_MXU), w_ref[...].astype(_MXU)) + b_ref[...]


def _modulation(c, ada_w, ada_b):
    depth = ada_w.shape[0]
    batch = c.shape[0]
    out = pl.pallas_call(
        _mod_kernel,
        out_shape=jax.ShapeDtypeStruct((depth, N_MOD, batch, D_MODEL), _F32),
        grid=(depth, N_MOD),
        in_specs=[
            pl.BlockSpec((batch, D_MODEL), lambda l, j: (0, 0)),
            pl.BlockSpec((None, D_MODEL, D_MODEL), lambda l, j: (l, 0, j)),
            pl.BlockSpec((None, None, 1, D_MODEL), lambda l, j: (l, j, 0, 0)),
        ],
        out_specs=pl.BlockSpec((None, None, batch, D_MODEL), lambda l, j: (l, j, 0, 0)),
        compiler_params=_params(("arbitrary", "arbitrary")),
        name="adaln_mod",
    )(c, ada_w, ada_b.reshape(depth, N_MOD, 1, D_MODEL))
    return jnp.transpose(out, (0, 2, 1, 3))


def _modulated_norm(x, mod_ref, gain_ref, piece):
    sh = mod_ref[3 * piece:3 * piece + 1, :]
    sc = mod_ref[3 * piece + 1:3 * piece + 2, :]
    return _rms(x, gain_ref[...]) * (1.0 + sc) + sh


def _ffn_kernel(x_ref, mod_ref, gain_ref, win_ref, wout_ref, *rest, piece, final):
    o_ref = rest[-1]
    g = mod_ref[3 * piece + 2:3 * piece + 3, :]
    subs = _sub_tiles(FFN_ROWS)
    hbs = [_modulated_norm(x_ref[rows, :], mod_ref, gain_ref, piece).astype(_MXU) for rows in subs]
    for rows, hb in zip(subs, hbs):
        acc = None
        for lo, hi in zip(FF_SPLITS[:-1], FF_SPLITS[1:]):
            gate = _dot(hb, win_ref[:, lo:hi])
            up = _dot(hb, win_ref[:, D_FF + lo:D_FF + hi])
            act = (gate * _sigmoid(gate) * up).astype(_MXU)
            part = _dot(act, wout_ref[lo:hi, :])
            acc = part if acc is None else acc + part
        y = x_ref[rows, :] + (0.5 * g) * acc
        if final:
            y = _rms(y, rest[0][...])
        o_ref[rows, :] = y


def _ffn(x, layer, mod, gain, w_in, w_out, piece, final_gain=None):
    batch, seq, _ = x.shape
    final = final_gain is not None
    in_specs = [
        pl.BlockSpec((None, FFN_ROWS, D_MODEL), lambda b, i: (b, i, 0)),
        _mod_spec(layer),
        _resident((1, D_MODEL), layer),
        _resident((D_MODEL, 2 * D_FF), layer),
        _resident((D_FF, D_MODEL), layer),
    ]
    args = [x, mod, gain, w_in, w_out]
    if final:
        in_specs.append(_resident((1, D_MODEL)))
        args.append(final_gain.reshape(1, D_MODEL))
    return pl.pallas_call(
        functools.partial(_ffn_kernel, piece=piece, final=final),
        out_shape=jax.ShapeDtypeStruct(x.shape, _F32),
        grid=(batch, seq // FFN_ROWS),
        in_specs=in_specs,
        out_specs=pl.BlockSpec((None, FFN_ROWS, D_MODEL), lambda b, i: (b, i, 0)),
        compiler_params=_params(("arbitrary", "arbitrary")),
        name="ffn_final" if final else "ffn",
    )(*args)


def _next_tile(b, i, batch, tiles):
    flat = jnp.minimum(b * tiles + i + 1, batch * tiles - 1)
    return flat // tiles, flat % tiles


def _next_specs(layer, batch, tiles, rows):
    def x_map(b, i):
        nb, ni = _next_tile(b, i, batch, tiles)
        return nb, ni, 0

    def mod_map(b, i):
        nb, _ = _next_tile(b, i, batch, tiles)
        return layer, nb, 0, 0

    return (pl.BlockSpec((None, rows, D_MODEL), x_map),
            pl.BlockSpec((None, None, N_MOD, D_MODEL), mod_map))


class _Tasks:
    def __init__(self, pieces, steps):
        self._pieces = iter(pieces)
        self._per_step = -(-len(pieces) // steps)

    def run(self, k=None):
        for _ in range(self._per_step if k is None else k):
            piece = next(self._pieces, None)
            if piece is not None:
                piece()

    def finish(self):
        for piece in self._pieces:
            piece()


def _pipelined(first_pieces, next_pieces, work, steps):
    b = pl.program_id(0)
    i = pl.program_id(1)

    @pl.when((b == 0) & (i == 0))
    def _():
        for piece in first_pieces(0):
            piece()

    for parity in (0, 1):
        @pl.when(i % 2 == parity)
        def _(parity=parity):
            tasks = _Tasks(next_pieces(1 - parity), steps)
            work(parity, tasks)
            tasks.finish()


def _col_chunks(lo, hi):
    return [(c, min(c + PROJ_COLS, hi)) for c in range(lo, hi, PROJ_COLS)]


def _swa_project_pieces(x_ref, mod_ref, gain_ref, w_ref, slot):
    q_s, k_s, ks_s, vt_s, gate_s = slot
    pieces = []
    for rows in _sub_tiles(SWA_ROWS, MIX_SUB):
        hb = []

        def norm(rows=rows, hb=hb):
            hb.append(_modulated_norm(x_ref[rows, :], mod_ref, gain_ref, 1).astype(_MXU))

        def kv_proj(rows=rows, hb=hb):
            kv = _dot(hb[0], w_ref[:, _A_KV:_A_G])
            k = kv[:, :ATT_KV_W] * (HEAD_DIM ** -0.5)
            k_s[rows, :] = k.astype(k_s.dtype)
            ks_s[rows, :] = pltpu.roll(k, HEAD_DIM, axis=1).astype(ks_s.dtype)
            vt_s[:, rows] = kv[:, ATT_KV_W:].T.astype(vt_s.dtype)

        def q_proj(lo, hi, rows=rows, hb=hb):
            q_s[rows, lo - _A_Q:hi - _A_Q] = _dot(hb[0], w_ref[:, lo:hi]).astype(q_s.dtype)

        def gate_proj(lo, hi, rows=rows, hb=hb):
            gate_s[rows, lo - _A_G:hi - _A_G] = _sigmoid(_dot(hb[0], w_ref[:, lo:hi])).astype(gate_s.dtype)

        pieces += [norm, kv_proj]
        pieces += [functools.partial(q_proj, lo, hi) for lo, hi in _col_chunks(_A_Q, _A_KV)]
        pieces += [functools.partial(gate_proj, lo, hi) for lo, hi in _col_chunks(_A_G, _A_END)]
    return pieces


SWA_STEPS = SWA_BLOCKS * N_KV_HEADS * 2
SWA_AHEAD = 2


def _swa_attend(sinks_ref, slot, carry, o_ref, sink_base, tasks):
    q_s, k_s, ks_s, vt_s, gate_s = slot
    ck, cks, cvt = carry
    n = pl.program_id(1)
    grp = N_Q_HEADS // N_KV_HEADS
    pairs = grp // 2
    band = 2 * SEQ_BLOCK

    k_all = jnp.concatenate([ck[...], k_s[...]], axis=0)
    ks_all = jnp.concatenate([cks[...], ks_s[...]], axis=0)
    vt_all = jnp.concatenate([cvt[...], vt_s[...]], axis=1)
    low = lax.broadcasted_iota(jnp.int32, (band, LANES), 1) < HEAD_DIM
    zero = jnp.zeros((band, LANES), k_all.dtype)
    ones = jnp.ones((HEAD_DIM, band), vt_all.dtype)

    kj = lax.broadcasted_iota(jnp.int32, (band, SEQ_BLOCK), 0)
    qi = lax.broadcasted_iota(jnp.int32, (band, SEQ_BLOCK), 1)
    in_window = (kj > qi) & (kj <= qi + WINDOW)

    def scores(j, kvh, parity):
        r0 = j * SEQ_BLOCK
        nat, swp = (k_all, ks_all) if kvh == 0 else (ks_all, k_all)
        if parity == 0:
            k_m = jnp.where(low, nat[r0:r0 + band], zero)
        else:
            k_m = jnp.where(low, zero, swp[r0:r0 + band])
        q = jnp.concatenate(
            [q_s[r0:r0 + SEQ_BLOCK, (kvh * pairs + p) * LANES:(kvh * pairs + p + 1) * LANES]
             for p in range(pairs)], axis=0)
        return _dot_nt(k_m, q)

    def attend(j, kvh, parity, s):
        r0 = j * SEQ_BLOCK
        valid = in_window if j > 0 else in_window & ((kj >= SEQ_BLOCK) | (n > 0))
        bias = jnp.where(valid, 0.0, -jnp.inf)
        v_h = vt_all[kvh * HEAD_DIM:(kvh + 1) * HEAD_DIM, r0:r0 + band]
        vt_m = jnp.concatenate([v_h, ones] if parity == 0 else [ones, v_h], axis=0)
        probs, tails = [], []
        for p in range(pairs):
            s_p = s[:, p * SEQ_BLOCK:(p + 1) * SEQ_BLOCK] + bias
            sink = jnp.full((1, SEQ_BLOCK), sinks_ref[sink_base + kvh * grp + 2 * p + parity], _F32)
            m = jnp.maximum(jnp.max(s_p, axis=0, keepdims=True), sink)
            probs.append(jnp.exp(s_p - m).astype(_MXU))
            tails.append(jnp.exp(sink - m))
        pv = _dot(vt_m, jnp.concatenate(probs, axis=1))
        if parity == 0:
            num, den = pv[:HEAD_DIM], pv[HEAD_DIM:HEAD_DIM + 1]
        else:
            num, den = pv[HEAD_DIM:], pv[:1]
        return num * (1.0 / (den + jnp.concatenate(tails, axis=1)))

    groups = [(j, kvh, parity) for j in range(SWA_BLOCKS) for kvh in range(N_KV_HEADS) for parity in range(2)]
    ahead = [scores(*g) for g in groups[:SWA_AHEAD]]
    halves = []
    for idx, (j, kvh, parity) in enumerate(groups):
        s = ahead.pop(0)
        if idx + SWA_AHEAD < len(groups):
            ahead.append(scores(*groups[idx + SWA_AHEAD]))
        halves.append(attend(j, kvh, parity, s))
        tasks.run()
        if parity == 1:
            out_t = jnp.concatenate(halves, axis=0)
            halves = []
            for p in range(pairs):
                rows = slice(j * SEQ_BLOCK, (j + 1) * SEQ_BLOCK)
                cols = slice((kvh * pairs + p) * LANES, (kvh * pairs + p + 1) * LANES)
                y = out_t[:, p * SEQ_BLOCK:(p + 1) * SEQ_BLOCK].T
                o_ref[rows, cols] = (gate_s[rows, cols].astype(_F32) * y).astype(o_ref.dtype)

    last = slice(SWA_ROWS - SEQ_BLOCK, SWA_ROWS)
    ck[...] = k_s[last, :]
    cks[...] = ks_s[last, :]
    cvt[...] = vt_s[:, last]


def _swa_kernel(sinks_ref, x0_ref, mod0_ref, xn_ref, modn_ref, gain_ref, w_ref, o_ref, *scratch, sink_base):
    slots = (scratch[0:5], scratch[5:10])
    carry = scratch[10:13]

    def zero_carry():
        for c in carry:
            c[...] = jnp.zeros_like(c)

    _pipelined(lambda s: [zero_carry] + _swa_project_pieces(x0_ref, mod0_ref, gain_ref, w_ref, slots[s]),
               lambda s: _swa_project_pieces(xn_ref, modn_ref, gain_ref, w_ref, slots[s]),
               lambda s, tasks: _swa_attend(sinks_ref, slots[s], carry, o_ref, sink_base, tasks),
               SWA_STEPS)


def _swa(x, layer, mod, gain, w_a, sinks):
    batch, seq, _ = x.shape
    tiles = seq // SWA_ROWS
    x_next, mod_next = _next_specs(layer, batch, tiles, SWA_ROWS)
    slot = [pltpu.VMEM((SWA_ROWS, ATT_Q_W), _MXU), pltpu.VMEM((SWA_ROWS, ATT_KV_W), _MXU),
            pltpu.VMEM((SWA_ROWS, ATT_KV_W), _MXU), pltpu.VMEM((ATT_KV_W, SWA_ROWS), _MXU),
            pltpu.VMEM((SWA_ROWS, D_MODEL), _MXU)]
    carry = [pltpu.VMEM((SEQ_BLOCK, ATT_KV_W), _MXU), pltpu.VMEM((SEQ_BLOCK, ATT_KV_W), _MXU),
             pltpu.VMEM((ATT_KV_W, SEQ_BLOCK), _MXU)]
    return pl.pallas_call(
        functools.partial(_swa_kernel, sink_base=layer * N_Q_HEADS),
        out_shape=jax.ShapeDtypeStruct((batch, seq, D_MODEL), _MXU),
        grid=(batch, tiles),
        in_specs=[
            pl.BlockSpec(memory_space=pltpu.SMEM),
            pl.BlockSpec((None, SWA_ROWS, D_MODEL), lambda b, i: (0, 0, 0)),
            pl.BlockSpec((None, None, N_MOD, D_MODEL), lambda b, i: (layer, 0, 0, 0)),
            x_next, mod_next,
            _resident((1, D_MODEL), layer),
            _resident((D_MODEL, _A_END), layer),
        ],
        out_specs=pl.BlockSpec((None, SWA_ROWS, D_MODEL), lambda b, i: (b, i, 0)),
        scratch_shapes=slot + slot + carry,
        compiler_params=_params(("arbitrary", "arbitrary")),
        name="swa",
    )(sinks, x, mod, x, mod, gain, w_a)


def _gla_project_pieces(x_ref, mod_ref, gain_ref, w_ref, wg_ref, bg_ref, slot):
    q_s, k_s, g_s, v_s, sr_s, sgb_s = slot
    pieces = []
    for rows in _sub_tiles(MIX_ROWS, MIX_SUB):
        hb = []

        def norm(rows=rows, hb=hb):
            hb.append(_modulated_norm(x_ref[rows, :], mod_ref, gain_ref, 1).astype(_MXU))

        def decay_gate(rows=rows, hb=hb):
            lr = _dot(hb[0], w_ref[:, _B_LR:_B_END]).astype(_MXU)
            z = _dot(lr, wg_ref[...]) + bg_ref[...]
            log_sig = jnp.minimum(z, 0.0) - jnp.log1p(jnp.exp(-jnp.abs(z)))
            g_s[rows, :] = log_sig / GLA_GATE_NORMALIZER

        def proj(dst, base, act, lo, hi, rows=rows, hb=hb):
            dst[rows, lo - base:hi - base] = act(_dot(hb[0], w_ref[:, lo:hi])).astype(dst.dtype)

        pieces += [norm, decay_gate]
        for dst, lo, hi, act in ((sr_s, _B_R, _B_G, lambda r: r * _sigmoid(r)),
                                 (sgb_s, _B_G, _B_LR, _sigmoid),
                                 (q_s, _B_Q, _B_K, lambda t: t),
                                 (k_s, _B_K, _B_V, lambda t: t),
                                 (v_s, _B_V, _B_R, lambda t: t)):
            pieces += [functools.partial(proj, dst, lo, act, c0, c1) for c0, c1 in _col_chunks(lo, hi)]
    return pieces


def _rows_bcast(x, starts, reps):
    return jnp.concatenate(
        [jnp.broadcast_to(x[s:s + 1, :], (reps, x.shape[1])) for s in starts], axis=0)


GLA_STEPS = MIX_BLOCKS * GLA_HEADS
GLA_AHEAD = 1


def _gla_attend(slot, gain_ref, za_ref, x_ref, mod_ref, w_ref, o_ref, state_ref, merged_ref, tasks):
    q_ref, k_ref, g_ref, v_ref, sr_ref, sgb_ref = slot
    C = SEQ_BLOCK
    ri = lax.broadcasted_iota(jnp.int32, (C, C), 0)
    ci = lax.broadcasted_iota(jnp.int32, (C, C), 1)
    tri = jnp.where(ci <= ri, 1.0, 0.0).astype(_MXU)
    kzero = jnp.zeros((C, GLA_DK), _MXU)
    diag_mask = (ri // GLA_SUB == ci // GLA_SUB) & (ci <= ri)
    levels = []
    b = GLA_SUB
    while b < C:
        pair_mask = (ri // (2 * b) == ci // (2 * b)) & ((ri // b) % 2 == 1) & ((ci // b) % 2 == 0)
        right = (lax.broadcasted_iota(jnp.int32, (C, GLA_DK), 0) // b) % 2 == 1
        levels.append((b, pair_mask, right))
        b *= 2

    def cum_decay(c):
        g = g_ref[c * C:(c + 1) * C, :] * LOG2_E
        g1 = g.astype(_MXU)
        r1 = g - g1.astype(_F32)
        g2 = r1.astype(_MXU)
        g3 = (r1 - g2.astype(_F32)).astype(_MXU)
        return _dot(tri, g1) + _dot(tri, g2) + _dot(tri, g3)

    def intra(c, h, G_all):
        rows = slice(c * C, (c + 1) * C)
        ks = slice(h * GLA_DK, (h + 1) * GLA_DK)
        G = G_all[:, ks]
        q = q_ref[rows, ks] * (GLA_DK ** -0.5)
        k = k_ref[rows, ks]
        G_last = G[C - 1:C, :]
        G_first = _rows_bcast(G, range(0, C, GLA_SUB), GLA_SUB)
        def t_mxu(x):
            return jnp.transpose(x).astype(_MXU)

        terms = [((q * jnp.exp2(G - G_first)).astype(_MXU), t_mxu(k * jnp.exp2(G_first - G)), diag_mask)]
        for b, pair_mask, right in levels:
            ref = _rows_bcast(G, range(b - 1, C, 2 * b), 2 * b)
            e = jnp.exp2(jnp.where(right, G - ref, ref - G))
            terms.append(((q * e).astype(_MXU), t_mxu(k * e), pair_mask))
        a = None
        for (q1, k1, m1), (q2, k2, m2) in zip(terms[0::2], terms[1::2]):
            lhs = jnp.concatenate([q1, q2], axis=1)
            rhs = jnp.concatenate([jnp.concatenate([k1, kzero], axis=1),
                                   jnp.concatenate([kzero, k2], axis=1)], axis=0)
            prod = _dot(lhs, rhs)
            part = jnp.where(m1, prod[:, :C], 0.0) + jnp.where(m2, prod[:, C:], 0.0)
            a = part if a is None else a + part
        qg = (q * jnp.exp2(G)).astype(_MXU)
        kg = (k * jnp.exp2(G_last - G)).astype(_MXU)
        return a.astype(_MXU), qg, kg, jnp.exp2(G_last)

    def inter(c, h, a, qg, kg, dec):
        rows = slice(c * C, (c + 1) * C)
        vs = slice(h * GLA_DV, (h + 1) * GLA_DV)
        v = v_ref[rows, vs]
        state = state_ref[h]
        lhs = jnp.concatenate([a, qg], axis=1)
        rhs = jnp.concatenate([v, state.astype(_MXU)], axis=0)
        o = _dot(lhs, rhs)
        upd = _dot_tn(kg, v)
        dec_t = jnp.transpose(jnp.broadcast_to(dec, (C, GLA_DK)))
        state_ref[h] = state * jnp.concatenate([dec_t] * (GLA_DV // GLA_DK), axis=1) + upd
        yb = _rms(o, gain_ref[...]) * sr_ref[rows, vs].astype(_F32)
        merged = za_ref[rows, vs].astype(_F32) + sgb_ref[rows, vs].astype(_F32) * yb
        merged_ref[rows, vs] = merged.astype(_MXU)

    units = [(c, h) for c in range(MIX_BLOCKS) for h in range(GLA_HEADS)]
    decays = {}

    def prepare(c, h):
        if c not in decays:
            decays[c] = cum_decay(c)
        return intra(c, h, decays[c])

    ahead = [prepare(*u) for u in units[:GLA_AHEAD]]
    for idx, (c, h) in enumerate(units):
        cur = ahead.pop(0)
        if idx + GLA_AHEAD < len(units):
            ahead.append(prepare(*units[idx + GLA_AHEAD]))
        inter(c, h, *cur)
        tasks.run()
        if h == GLA_HEADS - 1 and (c + 1) % GLA_PROJ_CHUNKS == 0:
            rows = slice((c + 1 - GLA_PROJ_CHUNKS) * C, (c + 1) * C)
            out = _dot(merged_ref[rows, :], w_ref[...])
            o_ref[rows, :] = x_ref[rows, :] + mod_ref[5:6, :] * out


def _gla_kernel(x_ref, mod_ref, xn_ref, modn_ref, gain_ref, w_ref, wg_ref, bg_ref, gla_gain_ref, za_ref,
                wout_ref, o_ref, *scratch):
    slots = (scratch[0:6], scratch[6:12])
    state_ref, merged_ref = scratch[12:14]

    @pl.when(pl.program_id(1) == 0)
    def _():
        state_ref[...] = jnp.zeros_like(state_ref)

    def pieces(src_ref, src_mod_ref, s):
        return _gla_project_pieces(src_ref, src_mod_ref, gain_ref, w_ref, wg_ref, bg_ref, slots[s])

    _pipelined(lambda s: pieces(x_ref, mod_ref, s),
               lambda s: pieces(xn_ref, modn_ref, s),
               lambda s, tasks: _gla_attend(slots[s], gla_gain_ref, za_ref, x_ref, mod_ref, wout_ref, o_ref,
                                            state_ref, merged_ref, tasks),
               GLA_STEPS)


def _gla_merge(x, layer, mod, gain, w_b, w_gate_p, b_gate, gla_gain, za, w_out):
    batch, seq, _ = x.shape
    tiles = seq // MIX_ROWS
    x_next, mod_next = _next_specs(layer, batch, tiles, MIX_ROWS)

    def tok(width):
        return pl.BlockSpec((None, MIX_ROWS, width), lambda b, i: (b, i, 0))

    slot = [pltpu.VMEM((MIX_ROWS, GLA_K_W), _F32), pltpu.VMEM((MIX_ROWS, GLA_K_W), _F32),
            pltpu.VMEM((MIX_ROWS, GLA_K_W), _F32), pltpu.VMEM((MIX_ROWS, GLA_V_W), _MXU),
            pltpu.VMEM((MIX_ROWS, GLA_V_W), _MXU), pltpu.VMEM((MIX_ROWS, D_MODEL), _MXU)]
    return pl.pallas_call(
        _gla_kernel,
        out_shape=jax.ShapeDtypeStruct(x.shape, _F32),
        grid=(batch, tiles),
        in_specs=[
            tok(D_MODEL), _mod_spec(layer), x_next, mod_next,
            _resident((1, D_MODEL), layer),
            _resident((D_MODEL, _B_END), layer),
            _resident((LANES, GLA_K_W), layer),
            _resident((1, GLA_K_W), layer),
            _resident((1, GLA_DV), layer),
            tok(D_MODEL),
            _resident((D_MODEL, D_MODEL), layer),
        ],
        out_specs=tok(D_MODEL),
        scratch_shapes=slot + slot + [pltpu.VMEM((GLA_HEADS, GLA_DK, GLA_DV), _F32),
                                      pltpu.VMEM((MIX_ROWS, D_MODEL), _MXU)],
        compiler_params=_params(("arbitrary", "arbitrary")),
        name="gla_merge",
    )(x, mod, x, mod, gain, w_b, w_gate_p, b_gate, gla_gain, za, w_out)


_IN_SIZES = (ATT_Q_W, ATT_KV_W, ATT_KV_W, GLA_K_W, GLA_K_W, GLA_V_W, GLA_RANK, GLA_V_W, D_MODEL, D_MODEL)
_IN_OFFS = tuple(sum(_IN_SIZES[:i]) for i in range(len(_IN_SIZES) + 1))
IN_COLS = _IN_OFFS[-1]
WPREP_ROWS = 256
WPREP_COLS = 512


def _wprep_kernel(wt_ref, wa_ref, wb_ref):
    o_qb, o_lr, o_r, o_ga, o_gb = _IN_OFFS[3], _IN_OFFS[6], _IN_OFFS[7], _IN_OFFS[8], _IN_OFFS[9]

    def cols(lo, width):
        return jnp.transpose(wt_ref[lo:lo + width, :])

    def put(dst, dst_lo, src_lo, width):
        for c in range(0, width, WPREP_COLS):
            n = min(WPREP_COLS, width - c)
            dst[:, dst_lo + c:dst_lo + c + n] = cols(src_lo + c, n).astype(_MXU)

    put(wa_ref, _A_Q, 0, o_qb)
    put(wa_ref, _A_G, o_ga, D_MODEL)
    put(wb_ref, _B_Q, o_qb, o_lr - o_qb)
    put(wb_ref, _B_R, o_r, GLA_V_W)
    put(wb_ref, _B_G, o_gb, D_MODEL)
    lr = cols(o_lr, LANES)
    keep = lax.broadcasted_iota(jnp.int32, lr.shape, 1) < GLA_RANK
    wb_ref[:, _B_LR:_B_END] = jnp.where(keep, lr, 0.0).astype(_MXU)


def _arrange_mix_w_in(w):
    depth = w.shape[0]
    assert w.shape[2] == IN_COLS and all(o % 8 == 0 for o in _IN_OFFS)
    return pl.pallas_call(
        _wprep_kernel,
        out_shape=[jax.ShapeDtypeStruct((depth, D_MODEL, _A_END), _MXU),
                   jax.ShapeDtypeStruct((depth, D_MODEL, _B_END), _MXU)],
        grid=(depth, D_MODEL // WPREP_ROWS),
        in_specs=[pl.BlockSpec((None, IN_COLS, WPREP_ROWS), lambda l, i: (l, 0, i))],
        out_specs=[pl.BlockSpec((None, WPREP_ROWS, _A_END), lambda l, i: (l, i, 0)),
                   pl.BlockSpec((None, WPREP_ROWS, _B_END), lambda l, i: (l, i, 0))],
        compiler_params=_params(("arbitrary", "arbitrary")),
        name="mix_weight_prep",
    )(jnp.swapaxes(w, 1, 2))


def kernel(x, c, ada_w, ada_b, norm1, ffn1_w_in, ffn1_w_out, norm_mix, mix_w_in, attn_sinks,
           gla_w_gate, gla_b_gate, gla_out_norm, mix_w_out, norm2, ffn2_w_in, ffn2_w_out, final_norm):
    depth = ada_w.shape[0]
    seq = x.shape[1]
    assert seq % FFN_ROWS == 0 and x.shape[2] == D_MODEL
    assert seq % (2 * MIX_ROWS) == 0 and seq % (2 * SWA_ROWS) == 0
    ffn_w = [(w_in.astype(_MXU), w_out.astype(_MXU))
             for w_in, w_out in ((ffn1_w_in, ffn1_w_out), (ffn2_w_in, ffn2_w_out))]
    ffn_gain = [n.reshape(depth, 1, D_MODEL) for n in (norm1, norm2)]
    mix_gain = norm_mix.reshape(depth, 1, D_MODEL)
    w_a, w_b = _arrange_mix_w_in(mix_w_in)
    w_gate_p = jnp.pad(gla_w_gate, ((0, 0), (0, LANES - GLA_RANK), (0, 0))).astype(_MXU)
    b_gate = gla_b_gate.reshape(depth, 1, GLA_K_W)
    gla_gain = gla_out_norm.reshape(depth, 1, GLA_DV)
    w_out = mix_w_out.astype(_MXU)
    sinks = attn_sinks.reshape(depth * N_Q_HEADS)

    mod = _modulation(c, ada_w, ada_b)
    for l in range(depth):
        x = _ffn(x, l, mod, ffn_gain[0], *ffn_w[0], 0)
        za = _swa(x, l, mod, mix_gain, w_a, sinks)
        x = _gla_merge(x, l, mod, mix_gain, w_b, w_gate_p, b_gate, gla_gain, za, w_out)
        x = _ffn(x, l, mod, ffn_gain[1], *ffn_w[1], 2,
                 final_gain=final_norm if l == depth - 1 else None)
    return x
```
